```python
import math
import functools
import jax
import jax.numpy as jnp
from jax import lax
import numpy as np

D_MODEL = 2048
BATCH = 2
SEQ = 8192
DEPTH = 4

GRID_W = 64
CTX_LEN = 256
D_FF = 5632
FFN_HALF = 0.5
EPS = 1e-6
CHUNK = 128
HG_CHUNK = 64
CONV_W = 5

SSD_D = D_MODEL // 2
SSD_HEAD_DIM = 64
SSD_HEADS = SSD_D // SSD_HEAD_DIM
SSD_GROUPS = 2
SSD_STATE = 128
SSD_XBC = SSD_D + 2 * SSD_GROUPS * SSD_STATE

ML_D = D_MODEL // 4
ML_HEADS = 4
ML_HEAD_DIM = ML_D // ML_HEADS

HG_D = D_MODEL // 4
HG_HEADS = 4
HG_DK = 128
HG_F = HG_HEADS * HG_DK
HG_DV = HG_D // HG_HEADS

D_MIX = SSD_D + ML_D + HG_D
IN_SPLITS = (SSD_D, SSD_XBC, 2 * SSD_HEADS, 2 * ML_D, ML_D, ML_D, 2 * ML_HEADS, 2 * ML_HEADS,
             HG_F, 2 * HG_F, HG_D, HG_D)
D_IN = sum(IN_SPLITS)

kernel_name = 'hybrid_ssd_mlstm_hgrn2_prefix_trunk'


def rmsnorm(x, w):
    xf = x.astype(jnp.float32)
    xf = xf * lax.rsqrt(jnp.mean(xf * xf, axis=-1, keepdims=True) + EPS)
    return (xf * w.astype(jnp.float32)).astype(x.dtype)


def group_rmsnorm(x, w, groups):
    shp = x.shape
    xg = x.reshape(*shp[:-1], groups, shp[-1] // groups)
    xg = xg * lax.rsqrt(jnp.mean(xg * xg, axis=-1, keepdims=True) + EPS)
    return xg.reshape(shp) * w.astype(jnp.float32)


def modulate(h, m):
    return h * (1 + m[:, 1, None]) + m[:, 0, None]


def swiglu(h, w_gate, w_up, w_down):
    return (jax.nn.silu(h @ w_gate) * (h @ w_up)) @ w_down


def ffn_sublayer(x, m, g, w_gate, w_up, w_down):
    return x + FFN_HALF * m[:, 2, None] * swiglu(modulate(rmsnorm(x, g), m), w_gate, w_up, w_down)


def dwconv(x, w, b):
    y = lax.conv_general_dilated(
        x, w[:, None, :].astype(x.dtype), window_strides=(1,),
        padding=[(CONV_W // 2, CONV_W // 2)], dimension_numbers=('NWC', 'WIO', 'NWC'),
        feature_group_count=x.shape[-1])
    return y + b.astype(x.dtype)


def to_col_major(a, rows):
    b, t = a.shape[:2]
    return a.reshape(b, rows, GRID_W, *a.shape[2:]).swapaxes(1, 2).reshape(b, t, *a.shape[2:])


def from_col_major(a, rows):
    b, t = a.shape[:2]
    return a.reshape(b, GRID_W, rows, *a.shape[2:]).swapaxes(1, 2).reshape(b, t, *a.shape[2:])


def flip_t(a):
    return jnp.flip(a, axis=1)


def causal_mask(n):
    return jnp.tril(jnp.ones((n, n), dtype=bool))


def chunked_scan(chunk_fn, seqs, state, chunk):
    b, t = seqs[0].shape[:2]
    n = t // chunk
    xs = tuple(s.reshape(b, n, chunk, *s.shape[2:]).swapaxes(0, 1) for s in seqs)
    state, ys = lax.scan(chunk_fn, state, xs)
    return ys.swapaxes(0, 1).reshape(b, t, *ys.shape[3:]), state


def ssd_chunk(a_head, h, xs):
    x, dt, bm, cm = xs
    rep = SSD_HEADS // SSD_GROUPS
    bh = jnp.repeat(bm, rep, axis=2)
    ch = jnp.repeat(cm, rep, axis=2)
    a = jnp.cumsum(dt * a_head, axis=1)
    seg = a[:, :, None, :] - a[:, None, :, :]
    decay = jnp.exp(jnp.where(causal_mask(x.shape[1])[None, :, :, None], seg, -jnp.inf))
    scores = jnp.einsum('bthn,bshn->btsh', ch, bh) * decay * dt[:, None, :, :]
    y = (jnp.einsum('btsh,bshp->bthp', scores, x)
         + jnp.einsum('bthn,bhnp->bthp', ch, h) * jnp.exp(a)[..., None])
    w = jnp.exp(a[:, -1:, :] - a) * dt
    h = h * jnp.exp(a[:, -1, :])[:, :, None, None] + jnp.einsum('bsh,bshn,bshp->bhnp', w, bh, x)
    return h, y


def mlstm_chunk(carry, xs):
    c_mem, n_mem, m = carry
    q, k, v, logi, logf = xs
    b = jnp.cumsum(logf, axis=1)
    dmat = b[:, :, None, :] - b[:, None, :, :] + logi[:, None, :, :]
    dmat = jnp.where(causal_mask(q.shape[1])[None, :, :, None], dmat, -jnp.inf)
    g = b + m[:, None, :]
    mt = jnp.maximum(g, jnp.max(dmat, axis=2))
    qk = jnp.einsum('bthd,bshd->btsh', q, k) * jnp.exp(dmat - mt[:, :, None, :])
    inter = jnp.exp(g - mt)
    num = (jnp.einsum('btsh,bshe->bthe', qk, v)
           + jnp.einsum('bthd,bhde->bthe', q, c_mem) * inter[..., None])
    den = jnp.sum(qk, axis=2) + jnp.einsum('bthd,bhd->bth', q, n_mem) * inter
    h = num / jnp.maximum(jnp.abs(den), jnp.exp(-mt))[..., None]
    bl = b[:, -1, :]
    ds = bl[:, None, :] - b + logi
    m_new = jnp.maximum(bl + m, jnp.max(ds, axis=1))
    ws = jnp.exp(ds - m_new[:, None, :])
    keep = jnp.exp(bl + m - m_new)
    c_mem = c_mem * keep[..., None, None] + jnp.einsum('bsh,bshd,bshe->bhde', ws, k, v)
    n_mem = n_mem * keep[..., None] + jnp.einsum('bsh,bshd->bhd', ws, k)
    return (c_mem, n_mem, m_new), h


def hgrn_chunk(s_mem, xs):
    q, logf, k, v = xs
    b = jnp.cumsum(logf, axis=1)
    seg = b[:, :, None] - b[:, None]
    decay = jnp.exp(jnp.where(causal_mask(q.shape[1])[None, :, :, None, None], seg, -jnp.inf))
    att = jnp.einsum('bthk,bshk,btshk->btsh', q, k, decay)
    o = (jnp.einsum('btsh,bshv->bthv', att, v)
         + jnp.einsum('bthk,bhkv->bthv', q * jnp.exp(b), s_mem))
    bl = b[:, -1]
    s_mem = s_mem * jnp.exp(bl)[..., None] + jnp.einsum('bshk,bshv->bhkv', k * jnp.exp(bl[:, None] - b), v)
    return s_mem, o


def bidir_prefix_scan(fn_f, fn_b, ctx_f, lat_f, ctx_b, lat_b, state0, chunk):
    yc_f, s_f = chunked_scan(fn_f, ctx_f, state0, chunk)
    yl_f, _ = chunked_scan(fn_f, lat_f, s_f, chunk)
    yc_b, s_b = chunked_scan(fn_b, tuple(flip_t(a) for a in ctx_b), state0, chunk)
    yl_b, _ = chunked_scan(fn_b, tuple(flip_t(a) for a in lat_b), s_b, chunk)
    return yc_f + flip_t(yc_b), yl_f + flip_t(yl_b)


def hgrn_lower_bounds(logits):
    p = jax.nn.softmax(logits.astype(jnp.float32), axis=0)
    cum = jnp.cumsum(p, axis=0)
    return cum - cum[0]


def mixer_features(u, lb, ssd_conv_w, ssd_conv_b, ssd_dt_bias, ml_conv_w, ml_conv_b, ml_ig_b, ml_fg_b):
    b, t = u.shape[:2]
    f32 = jnp.float32
    idx = [int(i) for i in np.cumsum(IN_SPLITS)[:-1]]
    (z, xbc, dt_raw, ml_qk, ml_v, ml_o, ml_ig, ml_fg, hg_q, hg_f, hg_i, hg_g) = jnp.split(u, idx, axis=-1)
    xbc = jax.nn.silu(dwconv(xbc, ssd_conv_w, ssd_conv_b)).astype(f32)
    sx, sb, sc = jnp.split(xbc, [SSD_D, SSD_D + SSD_GROUPS * SSD_STATE], axis=-1)
    dt = jax.nn.softplus(dt_raw.astype(f32).reshape(b, t, 2, SSD_HEADS) + ssd_dt_bias)
    qk = jax.nn.silu(dwconv(ml_qk, ml_conv_w, ml_conv_b)).astype(f32)
    q = qk[..., :ML_D].reshape(b, t, ML_HEADS, ML_HEAD_DIM)
    k = qk[..., ML_D:].reshape(b, t, ML_HEADS, ML_HEAD_DIM) * ML_HEAD_DIM ** -0.5
    logi = ml_ig.astype(f32).reshape(b, t, 2, ML_HEADS) + ml_ig_b
    logf = jax.nn.log_sigmoid(ml_fg.astype(f32).reshape(b, t, 2, ML_HEADS) + ml_fg_b)
    lbh = lb.reshape(HG_HEADS, HG_DK)
    f_raw = hg_f.astype(f32).reshape(b, t, 2, HG_HEADS, HG_DK)
    hlogf = jnp.logaddexp(jnp.log(lbh), jnp.log1p(-lbh) + jax.nn.log_sigmoid(f_raw))
    hk = (1.0 - lbh) * jax.nn.sigmoid(-f_raw)
    return dict(
        z=z, sx=sx.reshape(b, t, SSD_HEADS, SSD_HEAD_DIM),
        sb=sb.reshape(b, t, SSD_GROUPS, SSD_STATE), sc=sc.reshape(b, t, SSD_GROUPS, SSD_STATE), dt=dt,
        q=q, k=k, v=ml_v.astype(f32).reshape(b, t, ML_HEADS, ML_HEAD_DIM), mo=ml_o, logi=logi, logf=logf,
        hq=hg_q.astype(f32).reshape(b, t, HG_HEADS, HG_DK), hlogf=hlogf, hk=hk,
        hv=hg_i.astype(f32).reshape(b, t, HG_HEADS, HG_DV), hg=hg_g)


def merge_heads(f, y_ssd, y_ml, y_hg, ssd_d, ssd_norm_w, ml_norm_w, hg_norm_w, w_out):
    b, t = y_ssd.shape[:2]
    f32 = jnp.float32
    ys = (y_ssd + ssd_d.astype(f32)[:, None] * f['sx']).reshape(b, t, SSD_D)
    ys = group_rmsnorm(ys * jax.nn.silu(f['z'].astype(f32)), ssd_norm_w, SSD_GROUPS)
    ym = group_rmsnorm(y_ml.reshape(b, t, ML_D), ml_norm_w, ML_HEADS) * jax.nn.sigmoid(f['mo'].astype(f32))
    yh = group_rmsnorm(y_hg.reshape(b, t, HG_D), hg_norm_w, HG_HEADS) * jax.nn.silu(f['hg'].astype(f32))
    y = jnp.concatenate([ys, ym, yh], axis=-1).astype(w_out.dtype)
    return y @ w_out


def token_mix(h, hc, rows, need_ctx, w_in, w_out, lb, ssd_conv_w, ssd_conv_b, ssd_a_log, ssd_dt_bias,
              ssd_d, ssd_norm_w, ml_conv_w, ml_conv_b, ml_ig_b, ml_fg_b, ml_norm_w, hg_norm_w):
    feats = functools.partial(mixer_features, lb=lb, ssd_conv_w=ssd_conv_w, ssd_conv_b=ssd_conv_b,
                              ssd_dt_bias=ssd_dt_bias, ml_conv_w=ml_conv_w, ml_conv_b=ml_conv_b,
                              ml_ig_b=ml_ig_b, ml_fg_b=ml_fg_b)
    fl = feats(h @ w_in)
    fc = feats(hc @ w_in)
    b = h.shape[0]
    f32 = jnp.float32
    a = -jnp.exp(ssd_a_log.astype(f32))
    ssd_in = lambda f, d: (f['sx'], f['dt'][:, :, d], f['sb'], f['sc'])
    h0 = jnp.zeros((b, SSD_HEADS, SSD_STATE, SSD_HEAD_DIM), f32)
    yc_s, yl_s = bidir_prefix_scan(functools.partial(ssd_chunk, a[0]), functools.partial(ssd_chunk, a[1]),
                                   ssd_in(fc, 0), ssd_in(fl, 0), ssd_in(fc, 1), ssd_in(fl, 1), h0, CHUNK)
    ml_in = lambda f, d: (f['q'], f['k'], f['v'], f['logi'][:, :, d], f['logf'][:, :, d])
    s0 = (jnp.zeros((b, ML_HEADS, ML_HEAD_DIM, ML_HEAD_DIM), f32),
          jnp.zeros((b, ML_HEADS, ML_HEAD_DIM), f32), jnp.zeros((b, ML_HEADS), f32))
    yc_m, yl_m = bidir_prefix_scan(mlstm_chunk, mlstm_chunk, ml_in(fc, 0), ml_in(fl, 0),
                                   ml_in(fc, 1), ml_in(fl, 1), s0, CHUNK)
    hg_in = lambda f, d: (f['hq'], f['hlogf'][:, :, d], f['hk'][:, :, d], f['hv'])
    hg_lat = lambda d: tuple(to_col_major(arr, rows) for arr in hg_in(fl, d))
    g0 = jnp.zeros((b, HG_HEADS, HG_DK, HG_DV), f32)
    yc_h, yl_h = bidir_prefix_scan(hgrn_chunk, hgrn_chunk, hg_in(fc, 0), hg_lat(0), hg_in(fc, 1), hg_lat(1),
                                   g0, HG_CHUNK)
    yl_h = from_col_major(yl_h, rows)
    y = merge_heads(fl, yl_s, yl_m, yl_h, ssd_d, ssd_norm_w, ml_norm_w, hg_norm_w, w_out)
    yc = merge_heads(fc, yc_s, yc_m, yc_h, ssd_d, ssd_norm_w, ml_norm_w, hg_norm_w, w_out) if need_ctx else None
    return y, yc


def setup_inputs(seed: int = 0) -> dict:
    key = jax.random.key(seed)
    ks = jax.random.split(key, 28)
    f32 = jnp.float32
    nrm = lambda k, shape, scale: scale * jax.random.normal(k, shape, f32)
    x = nrm(ks[0], (BATCH, SEQ, D_MODEL), 1.0)
    c = nrm(ks[1], (BATCH, D_MODEL), 1.0)
    ctx = nrm(ks[2], (BATCH, CTX_LEN, D_MODEL), 1.0)
    c_ctx = nrm(ks[3], (D_MODEL,), 1.0)
    ada_w = nrm(ks[4], (DEPTH, D_MODEL, 9 * D_MODEL), 0.5 * D_MODEL ** -0.5)
    ada_b = nrm(ks[5], (DEPTH, 9 * D_MODEL), 0.02)
    norm_w = 1.0 + nrm(ks[6], (DEPTH, 3, D_MODEL), 0.02)
    ffn_w_gate = nrm(ks[7], (DEPTH, 2, D_MODEL, D_FF), D_MODEL ** -0.5)
    ffn_w_up = nrm(ks[8], (DEPTH, 2, D_MODEL, D_FF), D_MODEL ** -0.5)
    ffn_w_down = nrm(ks[9], (DEPTH, 2, D_FF, D_MODEL), D_FF ** -0.5)
    w_in = nrm(ks[10], (DEPTH, D_MODEL, D_IN), D_MODEL ** -0.5)
    w_out = nrm(ks[11], (DEPTH, D_MIX, D_MODEL), D_MIX ** -0.5)
    ssd_conv_w = nrm(ks[12], (DEPTH, CONV_W, SSD_XBC), CONV_W ** -0.5)
    ssd_conv_b = nrm(ks[13], (DEPTH, SSD_XBC), 0.02)
    ssd_a_log = jnp.log(jax.random.uniform(ks[14], (DEPTH, 2, SSD_HEADS), f32, 1.0, 16.0))
    dt0 = jnp.exp(jax.random.uniform(ks[15], (DEPTH, 2, SSD_HEADS), f32, math.log(1e-3), math.log(1e-1)))
    ssd_dt_bias = dt0 + jnp.log(-jnp.expm1(-dt0))
    ssd_d = 1.0 + nrm(ks[16], (DEPTH, SSD_HEADS), 0.1)
    ssd_norm_w = 1.0 + nrm(ks[17], (DEPTH, SSD_D), 0.02)
    ml_conv_w = nrm(ks[18], (DEPTH, CONV_W, 2 * ML_D), CONV_W ** -0.5)
    ml_conv_b = nrm(ks[19], (DEPTH, 2 * ML_D), 0.02)
    ml_ig_b = nrm(ks[20], (DEPTH, 2, ML_HEADS), 0.1)
    ml_fg_b = jnp.linspace(3.0, 6.0, ML_HEADS, dtype=f32) + nrm(ks[21], (DEPTH, 2, ML_HEADS), 0.1)
    ml_norm_w = 1.0 + nrm(ks[22], (DEPTH, ML_D), 0.02)
    hg_lb_logits = nrm(ks[23], (DEPTH, HG_F), 0.1)
    hg_norm_w = 1.0 + nrm(ks[24], (DEPTH, HG_D), 0.02)
    final_norm_w = 1.0 + nrm(ks[25], (D_MODEL,), 0.02)
    return {'x': x, 'c': c, 'ctx': ctx, 'c_ctx': c_ctx, 'ada_w': ada_w, 'ada_b': ada_b, 'norm_w': norm_w,
            'ffn_w_gate': ffn_w_gate, 'ffn_w_up': ffn_w_up, 'ffn_w_down': ffn_w_down, 'w_in': w_in,
            'w_out': w_out, 'ssd_conv_w': ssd_conv_w, 'ssd_conv_b': ssd_conv_b, 'ssd_a_log': ssd_a_log,
            'ssd_dt_bias': ssd_dt_bias, 'ssd_d': ssd_d, 'ssd_norm_w': ssd_norm_w, 'ml_conv_w': ml_conv_w,
            'ml_conv_b': ml_conv_b, 'ml_ig_b': ml_ig_b, 'ml_fg_b': ml_fg_b, 'ml_norm_w': ml_norm_w,
            'hg_lb_logits': hg_lb_logits, 'hg_norm_w': hg_norm_w, 'final_norm_w': final_norm_w}


def reference(x, c, ctx, c_ctx, ada_w, ada_b, norm_w, ffn_w_gate, ffn_w_up, ffn_w_down, w_in, w_out,
              ssd_conv_w, ssd_conv_b, ssd_a_log, ssd_dt_bias, ssd_d, ssd_norm_w, ml_conv_w, ml_conv_b,
              ml_ig_b, ml_fg_b, ml_norm_w, hg_lb_logits, hg_norm_w, final_norm_w):
    rows = x.shape[1] // GRID_W
    lb_all = hgrn_lower_bounds(hg_lb_logits)
    xc = ctx
    for l in range(DEPTH):
        need_ctx = l < DEPTH - 1
        mod = (jax.nn.silu(c) @ ada_w[l] + ada_b[l]).reshape(-1, 3, 3, D_MODEL)
        mod_c = (jax.nn.silu(c_ctx) @ ada_w[l] + ada_b[l]).reshape(1, 3, 3, D_MODEL)
        x = ffn_sublayer(x, mod[:, 0], norm_w[l, 0], ffn_w_gate[l, 0], ffn_w_up[l, 0], ffn_w_down[l, 0])
        xc = ffn_sublayer(xc, mod_c[:, 0], norm_w[l, 0], ffn_w_gate[l, 0], ffn_w_up[l, 0], ffn_w_down[l, 0])
        h = modulate(rmsnorm(x, norm_w[l, 1]), mod[:, 1])
        hc = modulate(rmsnorm(xc, norm_w[l, 1]), mod_c[:, 1])
        y, yc = token_mix(h, hc, rows, need_ctx, w_in[l], w_out[l], lb_all[l], ssd_conv_w[l], ssd_conv_b[l],
                          ssd_a_log[l], ssd_dt_bias[l], ssd_d[l], ssd_norm_w[l], ml_conv_w[l], ml_conv_b[l],
                          ml_ig_b[l], ml_fg_b[l], ml_norm_w[l], hg_norm_w[l])
        x = x + mod[:, 1, 2, None] * y
        x = ffn_sublayer(x, mod[:, 2], norm_w[l, 2], ffn_w_gate[l, 1], ffn_w_up[l, 1], ffn_w_down[l, 1])
        if need_ctx:
            xc = xc + mod_c[:, 1, 2, None] * yc
            xc = ffn_sublayer(xc, mod_c[:, 2], norm_w[l, 2], ffn_w_gate[l, 1], ffn_w_up[l, 1], ffn_w_down[l, 1])
    return rmsnorm(x, final_norm_w)
```

```python
import functools

import numpy as np
import jax
import jax.numpy as jnp
from jax import lax
from jax.experimental import pallas as pl
from jax.experimental.pallas import tpu as pltpu

F32 = jnp.float32
BF16 = jnp.bfloat16

GRID_W = 64
FFN_HALF = 0.5
EPS = 1e-6
CHUNK = 128
HG_CHUNK = 64
CONV_W = 5

SSD_HEADS = 16
SSD_HEAD_DIM = 64
SSD_GROUPS = 2
SSD_STATE = 128
ML_HEADS = 4
ML_HEAD_DIM = 128
HG_HEADS = 4
HG_DK = 128
HG_DV = 128

SSD_D = SSD_HEADS * SSD_HEAD_DIM
SSD_XBC = SSD_D + 2 * SSD_GROUPS * SSD_STATE
ML_D = ML_HEADS * ML_HEAD_DIM
HG_F = HG_HEADS * HG_DK
HG_D = HG_HEADS * HG_DV

COL_XBC = 0
COL_HGI = 1536
COL_Z = 2048
COL_MQK = 3072
COL_HGF = 4096
COL_MV = 5120
COL_MO = 5632
COL_HQ = 6144
COL_HGG = 6656
COL_SM = 7168
U_COLS = 7680
SM_DT = 0
SM_IG = 2 * SSD_HEADS
SM_FG = 2 * SSD_HEADS + 2 * ML_HEADS

NEG = -1e30
VMEM_LIMIT_BYTES = 56 * 1024 * 1024


def _cparams(n_axes):
    return pltpu.CompilerParams(dimension_semantics=("arbitrary",) * n_axes,
                                vmem_limit_bytes=VMEM_LIMIT_BYTES)


def _sigmoid(x):
    return 1.0 / (1.0 + jnp.exp(-x))


def _silu(x):
    return x * _sigmoid(x)


def _log_sigmoid(x):
    return jnp.minimum(x, 0.0) - jnp.log1p(jnp.exp(-jnp.abs(x)))


def _softplus(x):
    return jnp.maximum(x, 0.0) + jnp.log1p(jnp.exp(-jnp.abs(x)))


def _dot(a, b):
    return jnp.dot(a, b, preferred_element_type=F32)


def _dot_nt(a, b):
    return lax.dot_general(a, b, (((1,), (1,)), ((), ())), preferred_element_type=F32)


def _dot_tn(a, b):
    return lax.dot_general(a, b, (((0,), (0,)), ((), ())), preferred_element_type=F32)


def _dot_sel(sel_bf16, x):
    hi = x.astype(BF16)
    r1 = x - hi.astype(F32)
    mid = r1.astype(BF16)
    lo = (r1 - mid.astype(F32)).astype(BF16)
    return _dot(sel_bf16, hi) + _dot(sel_bf16, mid) + _dot(sel_bf16, lo)


def _norm_mod(x, pk):
    ms = jnp.mean(x * x, axis=-1, keepdims=True)
    xn = (x * lax.rsqrt(ms + EPS)) * pk[3:4]
    return xn * (1.0 + pk[1:2]) + pk[0:1]


def _ada_kernel(c_ref, w_ref, b_ref, o_ref):
    c = _silu(c_ref[...]).astype(BF16)
    o_ref[0] = _dot(c, w_ref[0].astype(BF16)) + b_ref[0]


def _ada_mods(cvec, ada_w, ada_b):
    depth, d, n = ada_w.shape
    tn = 1024
    return pl.pallas_call(
        _ada_kernel,
        grid=(depth, n // tn),
        in_specs=[pl.BlockSpec((8, d), lambda l, j: (0, 0)),
                  pl.BlockSpec((1, d, tn), lambda l, j: (l, 0, j)),
                  pl.BlockSpec((1, 1, tn), lambda l, j: (l, 0, j))],
        out_specs=pl.BlockSpec((1, 8, tn), lambda l, j: (l, 0, j)),
        out_shape=jax.ShapeDtypeStruct((depth, 8, n), F32),
        compiler_params=_cparams(2),
        name="ada_mods",
    )(cvec, ada_w, ada_b.reshape(depth, 1, n))


def _ffn_kernel(x_ref, pk_ref, wg_ref, wu_ref, wd_ref, fw_ref, o_ref, h_scr, acc_scr, *, nf, final):
    f = pl.program_id(1)

    @pl.when(f == 0)
    def _():
        h_scr[...] = _norm_mod(x_ref[...], pk_ref[0]).astype(BF16)
        acc_scr[...] = jnp.zeros_like(acc_scr)

    h = h_scr[...]
    g = _dot(h, wg_ref[...])
    u = _dot(h, wu_ref[...])
    a = (_silu(g) * u).astype(BF16)
    acc_scr[...] += _dot(a, wd_ref[...])

    @pl.when(f == nf - 1)
    def _():
        y = x_ref[...] + (FFN_HALF * pk_ref[0][2:3]) * acc_scr[...]
        if final:
            ms = jnp.mean(y * y, axis=-1, keepdims=True)
            y = (y * lax.rsqrt(ms + EPS)) * fw_ref[...]
        o_ref[...] = y


def _ffn(x, pack, stream_of_tile, wg, wu, wd, fw, *, tm, tf, final=False):
    m, d = x.shape
    fdim = wg.shape[1]
    nf = fdim // tf
    return pl.pallas_call(
        functools.partial(_ffn_kernel, nf=nf, final=final),
        grid=(m // tm, nf),
        in_specs=[pl.BlockSpec((tm, d), lambda i, f: (i, 0)),
                  pl.BlockSpec((1, 8, d), lambda i, f: (stream_of_tile(i), 0, 0)),
                  pl.BlockSpec((d, tf), lambda i, f: (0, f)),
                  pl.BlockSpec((d, tf), lambda i, f: (0, f)),
                  pl.BlockSpec((tf, d), lambda i, f: (f, 0)),
                  pl.BlockSpec((1, d), lambda i, f: (0, 0))],
        out_specs=pl.BlockSpec((tm, d), lambda i, f: (i, 0)),
        out_shape=jax.ShapeDtypeStruct((m, d), F32),
        scratch_shapes=[pltpu.VMEM((tm, d), BF16), pltpu.VMEM((tm, d), F32)],
        compiler_params=_cparams(2),
        name="ffn",
    )(x, pack, wg, wu, wd, fw)


def _proj_kernel(x_ref, pk_ref, w_ref, o_ref, h_scr):
    @pl.when(pl.program_id(1) == 0)
    def _():
        h_scr[...] = _norm_mod(x_ref[...], pk_ref[0]).astype(BF16)

    o_ref[...] = _dot(h_scr[...], w_ref[...])


def _proj(x, pack, stream_of_tile, w, *, tm, tn):
    m, d = x.shape
    n = w.shape[1]
    return pl.pallas_call(
        _proj_kernel,
        grid=(m // tm, n // tn),
        in_specs=[pl.BlockSpec((tm, d), lambda i, j: (i, 0)),
                  pl.BlockSpec((1, 8, d), lambda i, j: (stream_of_tile(i), 0, 0)),
                  pl.BlockSpec((d, tn), lambda i, j: (0, j))],
        out_specs=pl.BlockSpec((tm, tn), lambda i, j: (i, j)),
        out_shape=jax.ShapeDtypeStruct((m, n), F32),
        scratch_shapes=[pltpu.VMEM((tm, d), BF16)],
        compiler_params=_cparams(2),
        name="in_proj",
    )(x, pack, w)


def _group_norm(y, w, group):
    outs = []
    for g in range(y.shape[1] // group):
        yg = y[:, g * group:(g + 1) * group]
        ms = jnp.mean(yg * yg, axis=-1, keepdims=True)
        outs.append(yg * lax.rsqrt(ms + EPS))
    return jnp.concatenate(outs, axis=1) * w


def _merge_kernel(x_ref, pk_ref, sf_ref, sb_ref, mf_ref, mb_ref, hf_ref, hb_ref, z_ref, mo_ref, hg_ref,
                  nw_ref, wo_ref, o_ref):
    nw = nw_ref[...]
    ys = (sf_ref[...] + sb_ref[...]) * _silu(z_ref[...])
    ys = _group_norm(ys, nw[:, :SSD_D], SSD_D // SSD_GROUPS)
    ym = _group_norm(mf_ref[...] + mb_ref[...], nw[:, SSD_D:SSD_D + ML_D], ML_HEAD_DIM) * _sigmoid(mo_ref[...])
    yh = _group_norm(hf_ref[...] + hb_ref[...], nw[:, SSD_D + ML_D:], HG_DV) * _silu(hg_ref[...])
    y = jnp.concatenate([ys, ym, yh], axis=1).astype(BF16)
    o_ref[...] = x_ref[...] + pk_ref[0][2:3] * _dot(y, wo_ref[...])


def _merge(x, pack, stream_of_tile, ys, ym, yh, u, nw, wo, *, tm):
    m, d = x.shape

    def rows(width, colblk):
        return pl.BlockSpec((tm, width), lambda i: (i, colblk))

    return pl.pallas_call(
        _merge_kernel,
        grid=(m // tm,),
        in_specs=[rows(d, 0),
                  pl.BlockSpec((1, 8, d), lambda i: (stream_of_tile(i), 0, 0)),
                  rows(SSD_D, 0), rows(SSD_D, 0), rows(ML_D, 0), rows(ML_D, 0), rows(HG_D, 0), rows(HG_D, 0),
                  rows(SSD_D, COL_Z // SSD_D), rows(ML_D, COL_MO // ML_D), rows(HG_D, COL_HGG // HG_D),
                  pl.BlockSpec((1, d), lambda i: (0, 0)),
                  pl.BlockSpec((d, d), lambda i: (0, 0))],
        out_specs=rows(d, 0),
        out_shape=jax.ShapeDtypeStruct((m, d), F32),
        compiler_params=_cparams(1),
        name="merge_out",
    )(x, pack, ys[0], ys[1], ym[0], ym[1], yh[0], yh[1], u, u, u, nw, wo)


def _conv_silu(x_ref, halo_ref, cw_ref, cb_ref, pad_scr, length):
    pad_scr[0:8, :] = halo_ref[0, 0, 0:8, :]
    pad_scr[8:8 + length, :] = x_ref[0]
    pad_scr[8 + length:16 + length, :] = halo_ref[0, 0, 8:16, :]
    acc = cb_ref[...] + cw_ref[0:1, :] * pad_scr[pl.ds(8 - CONV_W // 2, length), :]
    for j in range(1, CONV_W):
        acc = acc + cw_ref[j:j + 1, :] * pad_scr[pl.ds(8 - CONV_W // 2 + j, length), :]
    return _silu(acc)


def _scan_tri(length, rev):
    r = lax.broadcasted_iota(jnp.int32, (length, length), 0)
    c = lax.broadcasted_iota(jnp.int32, (length, length), 1)
    return (c >= r) if rev else (c <= r)


def _ssd_kernel(x_ref, halo_ref, sm_ref, cw_ref, cb_ref, hp_ref, dv_ref, h0_ref, y_ref, hout_ref,
                pad_scr, h_scr, *, rev, nch):
    c = pl.program_id(1)
    L = CHUNK
    P = SSD_HEAD_DIM

    @pl.when(c == 0)
    def _():
        h_scr[...] = h0_ref[0]

    xbc = _conv_silu(x_ref, halo_ref, cw_ref, cb_ref, pad_scr, L)
    xs = xbc[:, :SSD_D]
    bm = xbc[:, SSD_D:SSD_D + SSD_GROUPS * SSD_STATE]
    cm = xbc[:, SSD_D + SSD_GROUPS * SSD_STATE:]

    off = SM_DT + SSD_HEADS * rev
    dt = _softplus(sm_ref[0][:, 0:128] + hp_ref[0:1, :])
    la = dt * hp_ref[1:2, :]
    mask = _scan_tri(L, rev)
    a = _dot_sel(mask.astype(BF16), la)
    a_end = a[0:1, :] if rev else a[L - 1:L, :]
    wfac = jnp.exp(a_end - a) * dt
    ea = jnp.exp(a)
    eend = jnp.exp(a_end)
    a_t = a.T
    dt_t = dt.T
    lane = lax.broadcasted_iota(jnp.int32, (1, 2 * P), 1)
    first = lane < P

    for g in range(SSD_GROUPS):
        cg = cm[:, g * SSD_STATE:(g + 1) * SSD_STATE]
        bg = bm[:, g * SSD_STATE:(g + 1) * SSD_STATE]
        cb = _dot_nt(cg.astype(BF16), bg.astype(BF16))
        for pp in range(SSD_HEADS // SSD_GROUPS // 2):
            pair = g * (SSD_HEADS // SSD_GROUPS // 2) + pp
            xp = xs[:, pair * 2 * P:(pair + 1) * 2 * P]
            hp = h_scr[pair]
            lhs, rhs, bws, xms, ends = [], [], [], [], []
            for k in range(2):
                ln = off + 2 * pair + k
                sel = first if k == 0 else jnp.logical_not(first)
                seg = a[:, ln:ln + 1] - a_t[ln:ln + 1, :]
                dec = jnp.exp(jnp.where(mask, seg, NEG))
                sc = cb * dec * dt_t[ln:ln + 1, :]
                lhs += [sc, cg * ea[:, ln:ln + 1]]
                xm = jnp.where(sel, xp, 0.0)
                rhs += [xm, jnp.where(sel, hp, 0.0)]
                bws.append(bg * wfac[:, ln:ln + 1])
                xms.append(xm)
                ends.append(eend[:, ln:ln + 1])
            y = _dot(jnp.concatenate(lhs, axis=1).astype(BF16), jnp.concatenate(rhs, axis=0).astype(BF16))
            if not rev:
                y = y + dv_ref[:, pair * 2 * P:(pair + 1) * 2 * P] * xp
            y_ref[0, :, pair * 2 * P:(pair + 1) * 2 * P] = y
            upd = _dot_tn(jnp.concatenate(bws, axis=0).astype(BF16), jnp.concatenate(xms, axis=0).astype(BF16))
            h_scr[pair] = hp * jnp.where(first, ends[0], ends[1]) + upd

    @pl.when(c == nch - 1)
    def _():
        hout_ref[0] = h_scr[...]


def _ssd_scan(u3, halo, cw, cb, hp, dv, h0, *, rev):
    b, t, _ = u3.shape
    nch = t // CHUNK
    ci = (lambda c: nch - 1 - c) if rev else (lambda c: c)
    npair = SSD_HEADS // 2
    st_shape = (b, npair, SSD_STATE, 2 * SSD_HEAD_DIM)
    return pl.pallas_call(
        functools.partial(_ssd_kernel, rev=rev, nch=nch),
        grid=(b, nch),
        in_specs=[pl.BlockSpec((1, CHUNK, SSD_XBC), lambda i, c: (i, ci(c), COL_XBC // SSD_XBC)),
                  pl.BlockSpec((1, 1, 16, SSD_XBC), lambda i, c: (i, ci(c), 0, 0)),
                  pl.BlockSpec((1, CHUNK, 512), lambda i, c: (i, ci(c), COL_SM // 512)),
                  pl.BlockSpec((8, SSD_XBC), lambda i, c: (0, 0)),
                  pl.BlockSpec((1, SSD_XBC), lambda i, c: (0, 0)),
                  pl.BlockSpec((8, 128), lambda i, c: (0, 0)),
                  pl.BlockSpec((1, SSD_D), lambda i, c: (0, 0)),
                  pl.BlockSpec((1,) + st_shape[1:], lambda i, c: (i, 0, 0, 0))],
        out_specs=[pl.BlockSpec((1, CHUNK, SSD_D), lambda i, c: (i, ci(c), 0)),
                   pl.BlockSpec((1,) + st_shape[1:], lambda i, c: (i, 0, 0, 0))],
        out_shape=[jax.ShapeDtypeStruct((b, t, SSD_D), F32), jax.ShapeDtypeStruct(st_shape, F32)],
        scratch_shapes=[pltpu.VMEM((CHUNK + 16, SSD_XBC), F32), pltpu.VMEM(st_shape[1:], F32)],
        compiler_params=_cparams(2),
        name="ssd_scan_bwd" if rev else "ssd_scan_fwd",
    )(u3, halo, u3, cw, cb, hp, dv, h0)


def _mlstm_kernel(qk_ref, halo_ref, v_ref, sm_ref, cw_ref, cb_ref, gb_ref, c0_ref, m0_ref,
                  y_ref, cout_ref, mout_ref, pad_scr, c_scr, m_scr, *, rev, nch):
    c = pl.program_id(1)
    L = CHUNK
    K = ML_HEAD_DIM

    @pl.when(c == 0)
    def _():
        c_scr[...] = c0_ref[0]
        m_scr[...] = m0_ref[0]

    qk = _conv_silu(qk_ref, halo_ref, cw_ref, cb_ref, pad_scr, L)
    raw = sm_ref[0][:, 0:128]
    logi = raw + gb_ref[0:1, :]
    logf = _log_sigmoid(raw + gb_ref[1:2, :])
    mask = _scan_tri(L, rev)
    bcum = _dot_sel(mask.astype(BF16), logf)
    b_t = bcum.T
    logi_t = logi.T
    lane = lax.broadcasted_iota(jnp.int32, (L, K), 1)
    ones_col = (lane == 0).astype(F32)
    scale = K ** -0.5
    end = 0 if rev else L - 1

    for h in range(ML_HEADS):
        li = SM_IG + ML_HEADS * rev + h
        lf = SM_FG + ML_HEADS * rev + h
        qh = qk[:, h * K:(h + 1) * K]
        kh = qk[:, ML_D + h * K:ML_D + (h + 1) * K] * scale
        vaug = jnp.concatenate([v_ref[0][:, h * K:(h + 1) * K], ones_col], axis=1).astype(BF16)
        m_old = m_scr[h][0:1, 0:1]
        bcol = bcum[:, lf:lf + 1]
        icol = logi[:, li:li + 1]
        dm = jnp.where(mask, bcol - b_t[lf:lf + 1, :] + logi_t[li:li + 1, :], NEG)
        gcol = bcol + m_old
        mt = jnp.maximum(gcol, jnp.max(dm, axis=1, keepdims=True))
        qkm = _dot_nt(qh.astype(BF16), kh.astype(BF16)) * jnp.exp(dm - mt)
        inter = jnp.exp(gcol - mt)
        lhs = jnp.concatenate([qkm, qh * inter], axis=1).astype(BF16)
        rhs = jnp.concatenate([vaug, c_scr[h].astype(BF16)], axis=0)
        s = _dot(lhs, rhs)
        den = jnp.maximum(jnp.abs(s[:, K:K + 1]), jnp.exp(-mt))
        y_ref[0, :, h * K:(h + 1) * K] = s[:, :K] / den
        bl = bcol[end:end + 1, :]
        ds = bl - bcol + icol
        m_new = jnp.maximum(bl + m_old, jnp.max(ds, axis=0, keepdims=True))
        ws = jnp.exp(ds - m_new)
        keep = jnp.exp(bl + m_old - m_new)
        c_scr[h] = c_scr[h] * keep + _dot_tn((kh * ws).astype(BF16), vaug)
        m_scr[h] = jnp.broadcast_to(m_new, (8, 128))

    @pl.when(c == nch - 1)
    def _():
        cout_ref[0] = c_scr[...]
        mout_ref[0] = m_scr[...]


def _mlstm_scan(u3, halo, cw, cb, gb, c0, m0, *, rev):
    b, t, _ = u3.shape
    nch = t // CHUNK
    ci = (lambda c: nch - 1 - c) if rev else (lambda c: c)
    c_shape = (b, ML_HEADS, ML_HEAD_DIM, 2 * ML_HEAD_DIM)
    m_shape = (b, ML_HEADS, 8, 128)
    return pl.pallas_call(
        functools.partial(_mlstm_kernel, rev=rev, nch=nch),
        grid=(b, nch),
        in_specs=[pl.BlockSpec((1, CHUNK, 2 * ML_D), lambda i, c: (i, ci(c), COL_MQK // (2 * ML_D))),
                  pl.BlockSpec((1, 1, 16, 2 * ML_D), lambda i, c: (i, ci(c), 0, 0)),
                  pl.BlockSpec((1, CHUNK, ML_D), lambda i, c: (i, ci(c), COL_MV // ML_D)),
                  pl.BlockSpec((1, CHUNK, 512), lambda i, c: (i, ci(c), COL_SM // 512)),
                  pl.BlockSpec((8, 2 * ML_D), lambda i, c: (0, 0)),
                  pl.BlockSpec((1, 2 * ML_D), lambda i, c: (0, 0)),
                  pl.BlockSpec((8, 128), lambda i, c: (0, 0)),
                  pl.BlockSpec((1,) + c_shape[1:], lambda i, c: (i, 0, 0, 0)),
                  pl.BlockSpec((1,) + m_shape[1:], lambda i, c: (i, 0, 0, 0))],
        out_specs=[pl.BlockSpec((1, CHUNK, ML_D), lambda i, c: (i, ci(c), 0)),
                   pl.BlockSpec((1,) + c_shape[1:], lambda i, c: (i, 0, 0, 0)),
                   pl.BlockSpec((1,) + m_shape[1:], lambda i, c: (i, 0, 0, 0))],
        out_shape=[jax.ShapeDtypeStruct((b, t, ML_D), F32), jax.ShapeDtypeStruct(c_shape, F32),
                   jax.ShapeDtypeStruct(m_shape, F32)],
        scratch_shapes=[pltpu.VMEM((CHUNK + 16, 2 * ML_D), F32), pltpu.VMEM(c_shape[1:], F32),
                        pltpu.VMEM(m_shape[1:], F32)],
        compiler_params=_cparams(2),
        name="mlstm_scan_bwd" if rev else "mlstm_scan_fwd",
    )(u3, halo, u3, u3, cw, cb, gb, c0, m0)


HG_LEVELS = 6
HG_SEL_ROWS = (2 + HG_LEVELS) * HG_CHUNK


def _hgrn_consts(rev):
    L = HG_CHUNK
    pos = np.arange(L)
    p = pos[::-1] if rev else pos
    cum = (p[None, :] <= p[:, None])
    after = (p[None, :] > p[:, None])
    sels = [cum, after]
    masks = []
    for lev in range(HG_LEVELS):
        m = 1 << lev
        blk = p // (2 * m)
        late = (p % (2 * m)) >= m
        ref = blk * 2 * m + m - 1
        pj = p[None, :]
        sel = np.where(late[:, None], (pj > ref[:, None]) & (pj <= p[:, None]),
                       (pj > p[:, None]) & (pj <= ref[:, None]))
        sels.append(sel)
        masks.append((blk[:, None] == blk[None, :]) & late[:, None] & (~late)[None, :])
    masks.append(np.eye(L, dtype=bool))
    sel_all = np.concatenate(sels, axis=0).astype(np.float32)
    mask_all = np.stack(masks, axis=0).astype(np.float32)
    return jnp.asarray(sel_all, BF16), jnp.asarray(mask_all, F32)


def _hgrn_kernel(q_ref, f_ref, v_ref, lb_ref, sel_ref, msk_ref, s0_ref, y_ref, sout_ref, s_scr, *, nch):
    c = pl.program_id(1)
    L = HG_CHUNK
    K = HG_DK

    @pl.when(c == 0)
    def _():
        s_scr[...] = s0_ref[0]

    lb = lb_ref[...]
    fr = f_ref[0]
    q = q_ref[0]
    v = v_ref[0]
    x1 = jnp.log(lb)
    x2 = jnp.log1p(-lb) + _log_sigmoid(fr)
    logf = jnp.maximum(x1, x2) + jnp.log1p(jnp.exp(-jnp.abs(x1 - x2)))
    k = (1.0 - lb) * _sigmoid(-fr)
    ex = _dot_sel(sel_ref[...], logf)
    bcum = ex[0:L]
    after = ex[L:2 * L]
    q_in = (q * jnp.exp(bcum)).astype(BF16)
    k_w = (k * jnp.exp(after)).astype(BF16)
    e_end = jnp.exp(bcum[0:1] + after[0:1])
    qb = q.astype(BF16)
    kb = k.astype(BF16)
    vb = v.astype(BF16)
    qs, ks = [], []
    for lev in range(HG_LEVELS):
        e = jnp.exp(ex[(2 + lev) * L:(3 + lev) * L])
        qs.append((q * e).astype(BF16))
        ks.append((k * e).astype(BF16))
    for h in range(HG_HEADS):
        sl = slice(h * K, (h + 1) * K)
        att = msk_ref[HG_LEVELS] * _dot_nt(qb[:, sl], kb[:, sl])
        for lev in range(HG_LEVELS):
            att = att + msk_ref[lev] * _dot_nt(qs[lev][:, sl], ks[lev][:, sl])
        st = s_scr[h]
        y_ref[0, :, sl] = _dot(att.astype(BF16), vb[:, sl]) + _dot_nt(q_in[:, sl], st.astype(BF16))
        s_scr[h] = st * e_end[:, sl] + _dot_tn(vb[:, sl], k_w[:, sl])

    @pl.when(c == nch - 1)
    def _():
        sout_ref[0] = s_scr[...]


def _hgrn_scan(u3, lb, s0, *, rev, col_major):
    b, r, wide = u3.shape
    w = wide // U_COLS
    per_col = r // HG_CHUNK
    nch = per_col * w
    ci = (lambda c: nch - 1 - c) if rev else (lambda c: c)
    nblk = U_COLS // HG_F

    def in_spec(colblk):
        return pl.BlockSpec((1, HG_CHUNK, HG_F),
                            lambda i, c: (i, ci(c) % per_col, (ci(c) // per_col) * nblk + colblk))

    sel, msk = _hgrn_consts(rev)
    s_shape = (b, HG_HEADS, HG_DV, HG_DK)
    return pl.pallas_call(
        functools.partial(_hgrn_kernel, nch=nch),
        grid=(b, nch),
        in_specs=[in_spec(COL_HQ // HG_F), in_spec(COL_HGF // HG_F + rev), in_spec(COL_HGI // HG_F),
                  pl.BlockSpec((1, HG_F), lambda i, c: (0, 0)),
                  pl.BlockSpec((HG_SEL_ROWS, HG_CHUNK), lambda i, c: (0, 0)),
                  pl.BlockSpec((HG_LEVELS + 1, HG_CHUNK, HG_CHUNK), lambda i, c: (0, 0, 0)),
                  pl.BlockSpec((1,) + s_shape[1:], lambda i, c: (i, 0, 0, 0))],
        out_specs=[pl.BlockSpec((1, HG_CHUNK, HG_D), lambda i, c: (i, ci(c) % per_col, ci(c) // per_col)),
                   pl.BlockSpec((1,) + s_shape[1:], lambda i, c: (i, 0, 0, 0))],
        out_shape=[jax.ShapeDtypeStruct((b, r, w * HG_D), F32), jax.ShapeDtypeStruct(s_shape, F32)],
        scratch_shapes=[pltpu.VMEM(s_shape[1:], F32)],
        compiler_params=_cparams(2),
        name="hgrn_scan_bwd" if rev else "hgrn_scan_fwd",
    )(u3, u3, u3, lb, sel, msk, s0)


def _conv_halo(u3, col0, width):
    b, t, _ = u3.shape
    nch = t // CHUNK
    u4 = u3.reshape(b, nch, CHUNK, U_COLS)
    head = u4[:, :, 0:2, col0:col0 + width]
    tail = u4[:, :, CHUNK - 2:CHUNK, col0:col0 + width]
    zero = jnp.zeros((b, 1, 2, width), F32)
    prev = jnp.concatenate([zero, tail[:, :-1]], axis=1)
    nxt = jnp.concatenate([head[:, 1:], zero], axis=1)
    pad = jnp.zeros((b, nch, 6, width), F32)
    return jnp.concatenate([pad, prev, nxt, pad], axis=2)


def _pad_rows(a, rows):
    return jnp.concatenate([a, jnp.zeros((rows - a.shape[0],) + a.shape[1:], a.dtype)], axis=0)


def _lane_vec(parts, width=128):
    out = jnp.zeros((width,), F32)
    for off, vals in parts:
        out = lax.dynamic_update_slice(out, vals.astype(F32), (off,))
    return out


def _permute_w_in(w_in):
    offs = np.cumsum([0, SSD_D, SSD_XBC, 2 * SSD_HEADS, 2 * ML_D, ML_D, ML_D, 2 * ML_HEADS, 2 * ML_HEADS,
                      HG_F, 2 * HG_F, HG_D, HG_D])
    z, xbc, dt, mqk, mv, mo, ig, fg, hq, hf, hi, hg = [w_in[:, offs[i]:offs[i + 1]] for i in range(12)]
    small = jnp.concatenate([dt, ig, fg], axis=1)
    small = jnp.concatenate([small, jnp.zeros((w_in.shape[0], 512 - small.shape[1]), w_in.dtype)], axis=1)
    return jnp.concatenate([xbc, hi, z, mqk, hf, mv, mo, hq, hg, small], axis=1).astype(BF16)


def kernel(x, c, ctx, c_ctx, ada_w, ada_b, norm_w, ffn_w_gate, ffn_w_up, ffn_w_down, w_in, w_out,
           ssd_conv_w, ssd_conv_b, ssd_a_log, ssd_dt_bias, ssd_d, ssd_norm_w, ml_conv_w, ml_conv_b,
           ml_ig_b, ml_fg_b, ml_norm_w, hg_lb_logits, hg_norm_w, final_norm_w):
    bsz, seq, d = x.shape
    ctx_len = ctx.shape[1]
    depth = ada_w.shape[0]
    rows = seq // GRID_W
    n_stream = bsz + 1

    pr = jax.nn.softmax(hg_lb_logits.astype(F32), axis=0)
    cum = jnp.cumsum(pr, axis=0)
    lb_all = cum - cum[0]

    cvec = _pad_rows(jnp.concatenate([c, c_ctx[None, :]], axis=0), 8)
    mods = _ada_mods(cvec, ada_w, ada_b)

    xl = x.reshape(bsz * seq, d)
    xc = ctx.reshape(bsz * ctx_len, d)
    tm_l = 512
    tm_c = min(512, bsz * ctx_len)
    lat_stream = lambda tm: (lambda i: i // (seq // tm))
    ctx_stream = lambda tm: (lambda i: bsz)
    fw = final_norm_w.reshape(1, d).astype(F32)
    zeros = lambda shape: jnp.zeros(shape, F32)

    for l in range(depth):
        need_ctx = l < depth - 1
        mod = mods[l, :n_stream].reshape(n_stream, 3, 3, d)

        def pack(sub):
            nw = jnp.broadcast_to(norm_w[l, sub][None, None, :], (n_stream, 1, d))
            return jnp.concatenate([mod[:, sub], nw, zeros((n_stream, 4, d))], axis=1)

        wg = ffn_w_gate[l].astype(BF16)
        wu = ffn_w_up[l].astype(BF16)
        wd = ffn_w_down[l].astype(BF16)

        pk0 = pack(0)
        xl = _ffn(xl, pk0, lat_stream(tm_l), wg[0], wu[0], wd[0], fw, tm=tm_l, tf=512)
        xc = _ffn(xc, pk0, ctx_stream(tm_c), wg[0], wu[0], wd[0], fw, tm=tm_c, tf=512)

        pk1 = pack(1)
        wp = _permute_w_in(w_in[l])
        ul = _proj(xl, pk1, lat_stream(tm_l), wp, tm=tm_l, tn=1536).reshape(bsz, seq, U_COLS)
        uc = _proj(xc, pk1, ctx_stream(tm_c), wp, tm=tm_c, tn=1536).reshape(bsz, ctx_len, U_COLS)

        a_neg = -jnp.exp(ssd_a_log[l].astype(F32))
        s_cw = _pad_rows(ssd_conv_w[l].astype(F32), 8)
        s_cb = ssd_conv_b[l].reshape(1, SSD_XBC).astype(F32)
        dv = jnp.repeat(ssd_d[l].astype(F32), SSD_HEAD_DIM).reshape(1, SSD_D)
        halo_c = _conv_halo(uc, COL_XBC, SSD_XBC)
        halo_l = _conv_halo(ul, COL_XBC, SSD_XBC)
        ys_l, ys_c = [], []
        for rev in (0, 1):
            off = SM_DT + SSD_HEADS * rev
            hp = _pad_rows(jnp.stack([_lane_vec([(off, ssd_dt_bias[l, rev])]),
                                      _lane_vec([(off, a_neg[rev])])]), 8)
            h0 = zeros((bsz, SSD_HEADS // 2, SSD_STATE, 2 * SSD_HEAD_DIM))
            yc_, hc = _ssd_scan(uc, halo_c, s_cw, s_cb, hp, dv, h0, rev=rev)
            yl_, _ = _ssd_scan(ul, halo_l, s_cw, s_cb, hp, dv, hc, rev=rev)
            ys_c.append(yc_.reshape(bsz * ctx_len, SSD_D))
            ys_l.append(yl_.reshape(bsz * seq, SSD_D))

        m_cw = _pad_rows(ml_conv_w[l].astype(F32), 8)
        m_cb = ml_conv_b[l].reshape(1, 2 * ML_D).astype(F32)
        halo_c = _conv_halo(uc, COL_MQK, 2 * ML_D)
        halo_l = _conv_halo(ul, COL_MQK, 2 * ML_D)
        ym_l, ym_c = [], []
        for rev in (0, 1):
            gb = _pad_rows(jnp.stack([_lane_vec([(SM_IG + ML_HEADS * rev, ml_ig_b[l, rev])]),
                                      _lane_vec([(SM_FG + ML_HEADS * rev, ml_fg_b[l, rev])])]), 8)
            c0 = zeros((bsz, ML_HEADS, ML_HEAD_DIM, 2 * ML_HEAD_DIM))
            m0 = zeros((bsz, ML_HEADS, 8, 128))
            yc_, cc, mc = _mlstm_scan(uc, halo_c, m_cw, m_cb, gb, c0, m0, rev=rev)
            yl_, _, _ = _mlstm_scan(ul, halo_l, m_cw, m_cb, gb, cc, mc, rev=rev)
            ym_c.append(yc_.reshape(bsz * ctx_len, ML_D))
            ym_l.append(yl_.reshape(bsz * seq, ML_D))

        lb = lb_all[l].reshape(1, HG_F)
        ul_cm = ul.reshape(bsz, rows, GRID_W * U_COLS)
        yh_l, yh_c = [], []
        for rev in (0, 1):
            s0 = zeros((bsz, HG_HEADS, HG_DV, HG_DK))
            yc_, sc = _hgrn_scan(uc, lb, s0, rev=rev, col_major=False)
            yl_, _ = _hgrn_scan(ul_cm, lb, sc, rev=rev, col_major=True)
            yh_c.append(yc_.reshape(bsz * ctx_len, HG_D))
            yh_l.append(yl_.reshape(bsz * seq, HG_D))

        nw = jnp.concatenate([ssd_norm_w[l], ml_norm_w[l], hg_norm_w[l]]).reshape(1, d).astype(F32)
        wo = w_out[l].astype(BF16)
        xl = _merge(xl, pk1, lat_stream(256), ys_l, ym_l, yh_l, ul.reshape(bsz * seq, U_COLS), nw, wo, tm=256)

        pk2 = pack(2)
        xl = _ffn(xl, pk2, lat_stream(tm_l), wg[1], wu[1], wd[1], fw, tm=tm_l, tf=512, final=not need_ctx)
        if need_ctx:
            xc = _merge(xc, pk1, ctx_stream(256), ys_c, ym_c, yh_c, uc.reshape(bsz * ctx_len, U_COLS), nw, wo,
                        tm=256)
            xc = _ffn(xc, pk2, ctx_stream(tm_c), wg[1], wu[1], wd[1], fw, tm=tm_c, tf=512)

    return xl.reshape(bsz, seq, d)
```

```python
import functools

import numpy as np
import jax
import jax.numpy as jnp
from jax import lax
from jax.experimental import pallas as pl
from jax.experimental.pallas import tpu as pltpu

F32 = jnp.float32
BF16 = jnp.bfloat16

GRID_W = 64
FFN_HALF = 0.5
EPS = 1e-6
CHUNK = 128
HG_CHUNK = 64
CONV_W = 5

SSD_HEADS = 16
SSD_HEAD_DIM = 64
SSD_GROUPS = 2
SSD_STATE = 128
ML_HEADS = 4
ML_HEAD_DIM = 128
HG_HEADS = 4
HG_DK = 128
HG_DV = 128

SSD_D = SSD_HEADS * SSD_HEAD_DIM
SSD_XBC = SSD_D + 2 * SSD_GROUPS * SSD_STATE
ML_D = ML_HEADS * ML_HEAD_DIM
HG_F = HG_HEADS * HG_DK
HG_D = HG_HEADS * HG_DV

COL_XBC = 0
COL_HGI = 1536
COL_Z = 2048
COL_MQK = 3072
COL_HGF = 4096
COL_MV = 5120
COL_MO = 5632
COL_HQ = 6144
COL_HGG = 6656
COL_SM = 7168
U_COLS = 7680
SM_DT = 0
SM_IG = 2 * SSD_HEADS
SM_FG = 2 * SSD_HEADS + 2 * ML_HEADS

NEG = -1e30
VMEM_LIMIT_BYTES = 56 * 1024 * 1024


def _cparams(n_axes):
    return pltpu.CompilerParams(dimension_semantics=("arbitrary",) * n_axes,
                                vmem_limit_bytes=VMEM_LIMIT_BYTES)


def _sigmoid(x):
    return 1.0 / (1.0 + jnp.exp(-x))


def _silu(x):
    return x * _sigmoid(x)


def _log_sigmoid(x):
    return jnp.minimum(x, 0.0) - jnp.log1p(jnp.exp(-jnp.abs(x)))


def _softplus(x):
    return jnp.maximum(x, 0.0) + jnp.log1p(jnp.exp(-jnp.abs(x)))


def _dot(a, b):
    return jnp.dot(a, b, preferred_element_type=F32)


def _dot_nt(a, b):
    return lax.dot_general(a, b, (((1,), (1,)), ((), ())), preferred_element_type=F32)


def _dot_tn(a, b):
    return lax.dot_general(a, b, (((0,), (0,)), ((), ())), preferred_element_type=F32)


def _dot_sel(sel_bf16, x):
    hi = x.astype(BF16)
    r1 = x - hi.astype(F32)
    mid = r1.astype(BF16)
    lo = (r1 - mid.astype(F32)).astype(BF16)
    return _dot(sel_bf16, hi) + _dot(sel_bf16, mid) + _dot(sel_bf16, lo)


def _norm_mod(x, pk):
    ms = jnp.mean(x * x, axis=-1, keepdims=True)
    xn = (x * lax.rsqrt(ms + EPS)) * pk[3:4]
    return xn * (1.0 + pk[1:2]) + pk[0:1]


def _ada_kernel(c_ref, w_ref, b_ref, o_ref):
    c = _silu(c_ref[...]).astype(BF16)
    o_ref[0] = _dot(c, w_ref[0].astype(BF16)) + b_ref[0]


def _ada_mods(cvec, ada_w, ada_b):
    depth, d, n = ada_w.shape
    tn = 1024
    return pl.pallas_call(
        _ada_kernel,
        grid=(depth, n // tn),
        in_specs=[pl.BlockSpec((8, d), lambda l, j: (0, 0)),
                  pl.BlockSpec((1, d, tn), lambda l, j: (l, 0, j)),
                  pl.BlockSpec((1, 1, tn), lambda l, j: (l, 0, j))],
        out_specs=pl.BlockSpec((1, 8, tn), lambda l, j: (l, 0, j)),
        out_shape=jax.ShapeDtypeStruct((depth, 8, n), F32),
        compiler_params=_cparams(2),
        name="ada_mods",
    )(cvec, ada_w, ada_b.reshape(depth, 1, n))


def _ffn_kernel(x_ref, pk_ref, wg_ref, wu_ref, wd_ref, fw_ref, o_ref, h_scr, acc_scr, *, nf, final):
    f = pl.program_id(1)

    @pl.when(f == 0)
    def _():
        h_scr[...] = _norm_mod(x_ref[...], pk_ref[0]).astype(BF16)
        acc_scr[...] = jnp.zeros_like(acc_scr)

    h = h_scr[...]
    g = _dot(h, wg_ref[...])
    u = _dot(h, wu_ref[...])
    a = (_silu(g) * u).astype(BF16)
    acc_scr[...] += _dot(a, wd_ref[...])

    @pl.when(f == nf - 1)
    def _():
        y = x_ref[...] + (FFN_HALF * pk_ref[0][2:3]) * acc_scr[...]
        if final:
            ms = jnp.mean(y * y, axis=-1, keepdims=True)
            y = (y * lax.rsqrt(ms + EPS)) * fw_ref[...]
        o_ref[...] = y


def _ffn(x, pack, stream_of_tile, wg, wu, wd, fw, *, tm, tf, final=False):
    m, d = x.shape
    fdim = wg.shape[1]
    nf = fdim // tf
    return pl.pallas_call(
        functools.partial(_ffn_kernel, nf=nf, final=final),
        grid=(m // tm, nf),
        in_specs=[pl.BlockSpec((tm, d), lambda i, f: (i, 0)),
                  pl.BlockSpec((1, 8, d), lambda i, f: (stream_of_tile(i), 0, 0)),
                  pl.BlockSpec((d, tf), lambda i, f: (0, f)),
                  pl.BlockSpec((d, tf), lambda i, f: (0, f)),
                  pl.BlockSpec((tf, d), lambda i, f: (f, 0)),
                  pl.BlockSpec((1, d), lambda i, f: (0, 0))],
        out_specs=pl.BlockSpec((tm, d), lambda i, f: (i, 0)),
        out_shape=jax.ShapeDtypeStruct((m, d), F32),
        scratch_shapes=[pltpu.VMEM((tm, d), BF16), pltpu.VMEM((tm, d), F32)],
        compiler_params=_cparams(2),
        name="ffn",
    )(x, pack, wg, wu, wd, fw)


def _proj_kernel(x_ref, pk_ref, w_ref, o_ref, h_scr):
    @pl.when(pl.program_id(1) == 0)
    def _():
        h_scr[...] = _norm_mod(x_ref[...], pk_ref[0]).astype(BF16)

    o_ref[...] = _dot(h_scr[...], w_ref[...])


def _proj(x, pack, stream_of_tile, w, *, tm, tn):
    m, d = x.shape
    n = w.shape[1]
    return pl.pallas_call(
        _proj_kernel,
        grid=(m // tm, n // tn),
        in_specs=[pl.BlockSpec((tm, d), lambda i, j: (i, 0)),
                  pl.BlockSpec((1, 8, d), lambda i, j: (stream_of_tile(i), 0, 0)),
                  pl.BlockSpec((d, tn), lambda i, j: (0, j))],
        out_specs=pl.BlockSpec((tm, tn), lambda i, j: (i, j)),
        out_shape=jax.ShapeDtypeStruct((m, n), F32),
        scratch_shapes=[pltpu.VMEM((tm, d), BF16)],
        compiler_params=_cparams(2),
        name="in_proj",
    )(x, pack, w)


def _group_norm(y, w, group):
    outs = []
    for g in range(y.shape[1] // group):
        yg = y[:, g * group:(g + 1) * group]
        ms = jnp.mean(yg * yg, axis=-1, keepdims=True)
        outs.append(yg * lax.rsqrt(ms + EPS))
    return jnp.concatenate(outs, axis=1) * w


def _merge_kernel(x_ref, pk_ref, sf_ref, sb_ref, mf_ref, mb_ref, hf_ref, hb_ref, z_ref, mo_ref, hg_ref,
                  nw_ref, wo_ref, o_ref):
    nw = nw_ref[...]
    ys = (sf_ref[...] + sb_ref[...]) * _silu(z_ref[...])
    ys = _group_norm(ys, nw[:, :SSD_D], SSD_D // SSD_GROUPS)
    ym = _group_norm(mf_ref[...] + mb_ref[...], nw[:, SSD_D:SSD_D + ML_D], ML_HEAD_DIM) * _sigmoid(mo_ref[...])
    yh = _group_norm(hf_ref[...] + hb_ref[...], nw[:, SSD_D + ML_D:], HG_DV) * _silu(hg_ref[...])
    y = jnp.concatenate([ys, ym, yh], axis=1).astype(BF16)
    o_ref[...] = x_ref[...] + pk_ref[0][2:3] * _dot(y, wo_ref[...])


def _merge(x, pack, stream_of_tile, ys, ym, yh, u, nw, wo, *, tm):
    m, d = x.shape

    def rows(width, colblk):
        return pl.BlockSpec((tm, width), lambda i: (i, colblk))

    return pl.pallas_call(
        _merge_kernel,
        grid=(m // tm,),
        in_specs=[rows(d, 0),
                  pl.BlockSpec((1, 8, d), lambda i: (stream_of_tile(i), 0, 0)),
                  rows(SSD_D, 0), rows(SSD_D, 0), rows(ML_D, 0), rows(ML_D, 0), rows(HG_D, 0), rows(HG_D, 0),
                  rows(SSD_D, COL_Z // SSD_D), rows(ML_D, COL_MO // ML_D), rows(HG_D, COL_HGG // HG_D),
                  pl.BlockSpec((1, d), lambda i: (0, 0)),
                  pl.BlockSpec((d, d), lambda i: (0, 0))],
        out_specs=rows(d, 0),
        out_shape=jax.ShapeDtypeStruct((m, d), F32),
        compiler_params=_cparams(1),
        name="merge_out",
    )(x, pack, ys[0], ys[1], ym[0], ym[1], yh[0], yh[1], u, u, u, nw, wo)


STEP_CHUNKS = 2


def _conv_silu(x, prev8, next8, cw_ref, cb_ref, pad_scr):
    rows, width = x.shape
    pad_scr[0:8, :] = prev8
    pad_scr[8:8 + rows, :] = x
    pad_scr[8 + rows:16 + rows, :] = next8
    xp = pad_scr[...]
    acc = None
    for j in range(CONV_W):
        sh = xp if j == CONV_W // 2 else pltpu.roll(xp, (CONV_W // 2 - j) % (rows + 16), axis=0)
        w8 = jnp.broadcast_to(cw_ref[j:j + 1, :], (8, width))
        term = sh[8:8 + rows].reshape(rows // 8, 8, width) * w8[None]
        acc = term if acc is None else acc + term
    acc = acc + jnp.broadcast_to(cb_ref[...], (8, width))[None]
    return _silu(acc).reshape(rows, width)


def _scan_tri(length, rev):
    r = lax.broadcasted_iota(jnp.int32, (length, length), 0)
    c = lax.broadcasted_iota(jnp.int32, (length, length), 1)
    return (c >= r) if rev else (c <= r)


def _scan_tri3(length, rev):
    hi = lax.broadcasted_iota(jnp.int32, (length // 8, 8, length), 0)
    lo = lax.broadcasted_iota(jnp.int32, (length // 8, 8, length), 1)
    c = lax.broadcasted_iota(jnp.int32, (length // 8, 8, length), 2)
    r = hi * 8 + lo
    return (c >= r) if rev else (c <= r)


def _tiles(x):
    return x.reshape(x.shape[0] // 8, 8, x.shape[1])


def _row8(x, i):
    return jnp.broadcast_to(x[i:i + 1, :], (8, x.shape[1]))[None]


def _halo_specs(width, colblk, rows, nsteps, step_of):
    per = rows // 8
    last = nsteps * per - 1
    prev = pl.BlockSpec((1, 8, width), lambda i, c: (i, jnp.maximum(step_of(c) * per - 1, 0), colblk))
    nxt = pl.BlockSpec((1, 8, width), lambda i, c: (i, jnp.minimum((step_of(c) + 1) * per, last), colblk))
    return prev, nxt


def _ssd_chunk(xbc, raw, hp, dv_ref, h_ref, y_ref, rs, rev):
    L = CHUNK
    P = SSD_HEAD_DIM
    xs = xbc[:, :SSD_D]
    bm = xbc[:, SSD_D:SSD_D + SSD_GROUPS * SSD_STATE]
    cm = xbc[:, SSD_D + SSD_GROUPS * SSD_STATE:]

    off = SM_DT + SSD_HEADS * rev
    dt = _softplus(raw + hp[0:1, :])
    la = dt * hp[1:2, :]
    a = _dot_sel(_scan_tri(L, rev).astype(BF16), la)
    mask = _scan_tri3(L, rev)
    a_end = a[0:1, :] if rev else a[L - 1:L, :]
    eend = jnp.exp(a_end)
    a_t = a.T
    dt_t = dt.T
    wf_t = (jnp.exp(a_end - a) * dt).T
    lane = lax.broadcasted_iota(jnp.int32, (1, 2 * P), 1)
    first = lane < P

    for g in range(SSD_GROUPS):
        cg = cm[:, g * SSD_STATE:(g + 1) * SSD_STATE]
        bg = bm[:, g * SSD_STATE:(g + 1) * SSD_STATE]
        cb = _tiles(_dot_nt(cg.astype(BF16), bg.astype(BF16)))
        cg3 = _tiles(cg)
        bg_t = _tiles(bg.T)
        for pp in range(SSD_HEADS // SSD_GROUPS // 2):
            pair = g * (SSD_HEADS // SSD_GROUPS // 2) + pp
            xp = xs[:, pair * 2 * P:(pair + 1) * 2 * P]
            hpair = h_ref[pair]
            lhs, rhs, blhs, xms, ends = [], [], [], [], []
            for k in range(2):
                ln = off + 2 * pair + k
                sel = first if k == 0 else jnp.logical_not(first)
                acol = _tiles(jnp.broadcast_to(a[:, ln:ln + 1], (L, L)))
                dec = jnp.exp(jnp.where(mask, acol - _row8(a_t, ln), NEG))
                lhs += [(cb * dec * _row8(dt_t, ln)).reshape(L, L), (cg3 * jnp.exp(acol)).reshape(L, SSD_STATE)]
                xm = jnp.where(sel, xp, 0.0)
                rhs += [xm, jnp.where(sel, hpair, 0.0)]
                blhs.append((bg_t * _row8(wf_t, ln)).reshape(SSD_STATE, L))
                xms.append(xm)
                ends.append(eend[:, ln:ln + 1])
            xcat = jnp.concatenate(xms, axis=0).astype(BF16)
            y = _dot(jnp.concatenate(lhs, axis=1).astype(BF16), jnp.concatenate(rhs, axis=0).astype(BF16))
            if not rev:
                y = y + dv_ref[:, pair * 2 * P:(pair + 1) * 2 * P] * xp
            y_ref[0, rs, pair * 2 * P:(pair + 1) * 2 * P] = y
            upd = _dot(jnp.concatenate(blhs, axis=1).astype(BF16), xcat)
            h_ref[pair] = hpair * jnp.where(first, ends[0], ends[1]) + upd


def _ssd_kernel(xf_ref, pf_ref, nf_ref, sf_ref, xb_ref, pb_ref, nb_ref, sb_ref, cw_ref, cb_ref, hp_ref, dv_ref,
                h0_ref, yf_ref, yb_ref, hout_ref, pad_scr, h_scr, *, nsteps):
    c = pl.program_id(1)

    @pl.when(c == 0)
    def _():
        h_scr[...] = h0_ref[:, 0]

    streams = ((xf_ref, pf_ref, nf_ref, sf_ref, yf_ref), (xb_ref, pb_ref, nb_ref, sb_ref, yb_ref))
    for d, (x_ref, p_ref, n_ref, s_ref, y_ref) in enumerate(streams):
        st = (nsteps - 1 - c) if d else c
        pv = (st > 0).astype(F32)
        nv = (st < nsteps - 1).astype(F32)
        xbc = _conv_silu(x_ref[0], p_ref[0] * pv, n_ref[0] * nv, cw_ref, cb_ref, pad_scr.at[d])
        sm = s_ref[0]
        subs = range(STEP_CHUNKS - 1, -1, -1) if d else range(STEP_CHUNKS)
        for u in subs:
            rs = slice(u * CHUNK, (u + 1) * CHUNK)
            _ssd_chunk(xbc[rs], sm[rs, 0:128], hp_ref[d], dv_ref, h_scr.at[d], y_ref, rs, d)

    @pl.when(c == nsteps - 1)
    def _():
        hout_ref[:, 0] = h_scr[...]


def _ssd_scan(u3, cw, cb, hp, dv, h0):
    b, t, _ = u3.shape
    rows = STEP_CHUNKS * CHUNK
    nsteps = t // rows
    fwd = lambda c: c
    bwd = lambda c: nsteps - 1 - c
    st_shape = h0.shape

    def stream_specs(step_of):
        return [pl.BlockSpec((1, rows, SSD_XBC), lambda i, c: (i, step_of(c), COL_XBC // SSD_XBC)),
                *_halo_specs(SSD_XBC, COL_XBC // SSD_XBC, rows, nsteps, step_of),
                pl.BlockSpec((1, rows, 512), lambda i, c: (i, step_of(c), COL_SM // 512))]

    st_spec = pl.BlockSpec((2, 1) + st_shape[2:], lambda i, c: (0, i, 0, 0, 0))
    return pl.pallas_call(
        functools.partial(_ssd_kernel, nsteps=nsteps),
        grid=(b, nsteps),
        in_specs=stream_specs(fwd) + stream_specs(bwd) + [
            pl.BlockSpec((8, SSD_XBC), lambda i, c: (0, 0)),
            pl.BlockSpec((1, SSD_XBC), lambda i, c: (0, 0)),
            pl.BlockSpec((2, 8, 128), lambda i, c: (0, 0, 0)),
            pl.BlockSpec((1, SSD_D), lambda i, c: (0, 0)),
            st_spec],
        out_specs=[pl.BlockSpec((1, rows, SSD_D), lambda i, c: (i, fwd(c), 0)),
                   pl.BlockSpec((1, rows, SSD_D), lambda i, c: (i, bwd(c), 0)),
                   st_spec],
        out_shape=[jax.ShapeDtypeStruct((b, t, SSD_D), F32), jax.ShapeDtypeStruct((b, t, SSD_D), F32),
                   jax.ShapeDtypeStruct(st_shape, F32)],
        scratch_shapes=[pltpu.VMEM((2, rows + 16, SSD_XBC), F32), pltpu.VMEM((2,) + st_shape[2:], F32)],
        compiler_params=_cparams(2),
        name="ssd_scan",
    )(u3, u3, u3, u3, u3, u3, u3, u3, cw, cb, hp, dv, h0)


def _mlstm_chunk(qk, v, raw, gb, c_ref, m_ref, y_ref, rs, rev):
    L = CHUNK
    K = ML_HEAD_DIM
    logi = raw + gb[0:1, :]
    logf = _log_sigmoid(raw + gb[1:2, :])
    bcum = _dot_sel(_scan_tri(L, rev).astype(BF16), logf)
    mask = _scan_tri3(L, rev)
    b_t = bcum.T
    logi_t = logi.T
    ones = jnp.ones((L, K), BF16)
    scale = K ** -0.5
    end = 0 if rev else L - 1

    for h in range(ML_HEADS):
        li = SM_IG + ML_HEADS * rev + h
        lf = SM_FG + ML_HEADS * rev + h
        qh = qk[:, h * K:(h + 1) * K]
        kh = qk[:, ML_D + h * K:ML_D + (h + 1) * K] * scale
        vaug = jnp.concatenate([v[:, h * K:(h + 1) * K].astype(BF16), ones], axis=1)
        m_old = m_ref[h][0:1, 0:1]
        brow = b_t[lf:lf + 1, :]
        irow = logi_t[li:li + 1, :]
        bcol = _tiles(jnp.broadcast_to(bcum[:, lf:lf + 1], (L, L)))
        dm = jnp.where(mask, bcol - jnp.broadcast_to(brow - irow, (8, L))[None], NEG)
        gfull = bcol + m_old
        mt = jnp.maximum(gfull, jnp.max(dm, axis=2, keepdims=True))
        qkm = _tiles(_dot_nt(qh.astype(BF16), kh.astype(BF16))) * jnp.exp(dm - mt)
        qin = _tiles(qh) * jnp.exp(gfull - mt)
        lhs = jnp.concatenate([qkm.reshape(L, L), qin.reshape(L, K)], axis=1).astype(BF16)
        rhs = jnp.concatenate([vaug, c_ref[h].astype(BF16)], axis=0)
        s = _dot(lhs, rhs)
        den = jnp.maximum(jnp.abs(s[:, K:]), jnp.exp(-mt).reshape(L, L))
        y_ref[0, rs, h * K:(h + 1) * K] = s[:, :K] / den
        bl = brow[:, end:end + 1]
        ds = bl - brow + irow
        m_new = jnp.maximum(bl + m_old, jnp.max(ds, axis=1, keepdims=True))
        ws = jnp.exp(ds - m_new)
        keep = jnp.exp(bl + m_old - m_new)
        c_ref[h] = c_ref[h] * keep + _dot((kh.T * ws).astype(BF16), vaug)
        m_ref[h] = jnp.broadcast_to(m_new, (8, 128))


def _mlstm_kernel(qf_ref, pf_ref, nf_ref, vf_ref, sf_ref, qb_ref, pb_ref, nb_ref, vb_ref, sb_ref,
                  cw_ref, cb_ref, gb_ref, c0_ref, m0_ref, yf_ref, yb_ref, cout_ref, mout_ref,
                  pad_scr, c_scr, m_scr, *, nsteps):
    c = pl.program_id(1)

    @pl.when(c == 0)
    def _():
        c_scr[...] = c0_ref[:, 0]
        m_scr[...] = m0_ref[:, 0]

    streams = ((qf_ref, pf_ref, nf_ref, vf_ref, sf_ref, yf_ref), (qb_ref, pb_ref, nb_ref, vb_ref, sb_ref, yb_ref))
    for d, (q_ref, p_ref, n_ref, v_ref, s_ref, y_ref) in enumerate(streams):
        st = (nsteps - 1 - c) if d else c
        pv = (st > 0).astype(F32)
        nv = (st < nsteps - 1).astype(F32)
        qk = _conv_silu(q_ref[0], p_ref[0] * pv, n_ref[0] * nv, cw_ref, cb_ref, pad_scr.at[d])
        v = v_ref[0]
        sm = s_ref[0]
        subs = range(STEP_CHUNKS - 1, -1, -1) if d else range(STEP_CHUNKS)
        for u in subs:
            rs = slice(u * CHUNK, (u + 1) * CHUNK)
            _mlstm_chunk(qk[rs], v[rs], sm[rs, 0:128], gb_ref[d], c_scr.at[d], m_scr.at[d], y_ref, rs, d)

    @pl.when(c == nsteps - 1)
    def _():
        cout_ref[:, 0] = c_scr[...]
        mout_ref[:, 0] = m_scr[...]


def _mlstm_scan(u3, cw, cb, gb, c0, m0):
    b, t, _ = u3.shape
    rows = STEP_CHUNKS * CHUNK
    nsteps = t // rows
    fwd = lambda c: c
    bwd = lambda c: nsteps - 1 - c

    def stream_specs(step_of):
        return [pl.BlockSpec((1, rows, 2 * ML_D), lambda i, c: (i, step_of(c), COL_MQK // (2 * ML_D))),
                *_halo_specs(2 * ML_D, COL_MQK // (2 * ML_D), rows, nsteps, step_of),
                pl.BlockSpec((1, rows, ML_D), lambda i, c: (i, step_of(c), COL_MV // ML_D)),
                pl.BlockSpec((1, rows, 512), lambda i, c: (i, step_of(c), COL_SM // 512))]

    c_spec = pl.BlockSpec((2, 1) + c0.shape[2:], lambda i, c: (0, i, 0, 0, 0))
    m_spec = pl.BlockSpec((2, 1) + m0.shape[2:], lambda i, c: (0, i, 0, 0, 0))
    return pl.pallas_call(
        functools.partial(_mlstm_kernel, nsteps=nsteps),
        grid=(b, nsteps),
        in_specs=stream_specs(fwd) + stream_specs(bwd) + [
            pl.BlockSpec((8, 2 * ML_D), lambda i, c: (0, 0)),
            pl.BlockSpec((1, 2 * ML_D), lambda i, c: (0, 0)),
            pl.BlockSpec((2, 8, 128), lambda i, c: (0, 0, 0)),
            c_spec, m_spec],
        out_specs=[pl.BlockSpec((1, rows, ML_D), lambda i, c: (i, fwd(c), 0)),
                   pl.BlockSpec((1, rows, ML_D), lambda i, c: (i, bwd(c), 0)),
                   c_spec, m_spec],
        out_shape=[jax.ShapeDtypeStruct((b, t, ML_D), F32), jax.ShapeDtypeStruct((b, t, ML_D), F32),
                   jax.ShapeDtypeStruct(c0.shape, F32), jax.ShapeDtypeStruct(m0.shape, F32)],
        scratch_shapes=[pltpu.VMEM((2, rows + 16, 2 * ML_D), F32), pltpu.VMEM((2,) + c0.shape[2:], F32),
                        pltpu.VMEM((2,) + m0.shape[2:], F32)],
        compiler_params=_cparams(2),
        name="mlstm_scan",
    )(u3, u3, u3, u3, u3, u3, u3, u3, u3, u3, cw, cb, gb, c0, m0)


HG_LEVELS = 6
HG_SEL_ROWS = (2 + HG_LEVELS) * HG_CHUNK


def _hgrn_consts(rev):
    L = HG_CHUNK
    pos = np.arange(L)
    p = pos[::-1] if rev else pos
    cum = (p[None, :] <= p[:, None])
    after = (p[None, :] > p[:, None])
    sels = [cum, after]
    masks = []
    for lev in range(HG_LEVELS):
        m = 1 << lev
        blk = p // (2 * m)
        late = (p % (2 * m)) >= m
        ref = blk * 2 * m + m - 1
        pj = p[None, :]
        sel = np.where(late[:, None], (pj > ref[:, None]) & (pj <= p[:, None]),
                       (pj > p[:, None]) & (pj <= ref[:, None]))
        sels.append(sel)
        masks.append((blk[:, None] == blk[None, :]) & late[:, None] & (~late)[None, :])
    masks.append(np.eye(L, dtype=bool))
    sel_all = np.concatenate(sels, axis=0).astype(np.float32)
    mask_all = np.stack(masks, axis=0).astype(np.float32)
    return jnp.asarray(sel_all, BF16), jnp.asarray(mask_all, F32)


def _hgrn_chunk(q, fr, v, lb, sel_ref, msk_ref, s_ref):
    L = HG_CHUNK
    K = HG_DK
    x1 = jnp.log(lb)
    x2 = jnp.log1p(-lb) + _log_sigmoid(fr)
    logf = jnp.maximum(x1, x2) + jnp.log1p(jnp.exp(-jnp.abs(x1 - x2)))
    k = (1.0 - lb) * _sigmoid(-fr)
    ex = _dot_sel(sel_ref[...], logf)
    bcum = ex[0:L]
    after = ex[L:2 * L]
    q_in = (q * jnp.exp(bcum)).astype(BF16)
    k_w = (k * jnp.exp(after)).astype(BF16)
    e_end = jnp.exp(bcum[0:1] + after[0:1])
    qb = q.astype(BF16)
    kb = k.astype(BF16)
    vb = v.astype(BF16)
    qs, ks = [], []
    for lev in range(HG_LEVELS):
        e = jnp.exp(ex[(2 + lev) * L:(3 + lev) * L])
        qs.append((q * e).astype(BF16))
        ks.append((k * e).astype(BF16))
    outs = []
    for h in range(HG_HEADS):
        sl = slice(h * K, (h + 1) * K)
        att = msk_ref[HG_LEVELS] * _dot_nt(qb[:, sl], kb[:, sl])
        for lev in range(HG_LEVELS):
            att = att + msk_ref[lev] * _dot_nt(qs[lev][:, sl], ks[lev][:, sl])
        st = s_ref[h]
        outs.append(_dot(att.astype(BF16), vb[:, sl]) + _dot_nt(q_in[:, sl], st.astype(BF16)))
        s_ref[h] = st * e_end[:, sl] + _dot_tn(vb[:, sl], k_w[:, sl])
    return jnp.concatenate(outs, axis=1)


def _hgrn_rows_kernel(q_ref, f_ref, v_ref, lb_ref, sel_ref, msk_ref, s0_ref, y_ref, sout_ref, s_scr,
                      *, nsteps, nsub, rev):
    c = pl.program_id(1)

    @pl.when(c == 0)
    def _():
        s_scr[...] = s0_ref[0]

    lb = lb_ref[...]
    for u in (range(nsub - 1, -1, -1) if rev else range(nsub)):
        rs = slice(u * HG_CHUNK, (u + 1) * HG_CHUNK)
        y_ref[0, rs, :] = _hgrn_chunk(q_ref[0, rs, :], f_ref[0, rs, :], v_ref[0, rs, :], lb, sel_ref, msk_ref, s_scr)

    @pl.when(c == nsteps - 1)
    def _():
        sout_ref[0] = s_scr[...]


def _hgrn_cols_kernel(q_ref, f_ref, v_ref, lb_ref, sel_ref, msk_ref, s0_ref, y_ref, sout_ref,
                      q_scr, f_scr, v_scr, y_scr, s_scr, *, nsteps, nsub, rev):
    c = pl.program_id(1)
    w = (nsteps - 1 - c) if rev else c
    wi = w % 8
    enter, leave = (7, 0) if rev else (0, 7)

    @pl.when(c == 0)
    def _():
        s_scr[...] = s0_ref[0]

    @pl.when(wi == enter)
    def _():
        for k in range(8):
            q_scr[k] = q_ref[0, :, 0, k, :]
            f_scr[k] = f_ref[0, :, 0, k, :]
            v_scr[k] = v_ref[0, :, 0, k, :]

    lb = lb_ref[...]
    for u in (range(nsub - 1, -1, -1) if rev else range(nsub)):
        rs = pl.ds(u * HG_CHUNK, HG_CHUNK)
        y_scr[wi, rs, :] = _hgrn_chunk(q_scr[wi, rs, :], f_scr[wi, rs, :], v_scr[wi, rs, :], lb, sel_ref, msk_ref,
                                       s_scr)

    @pl.when(wi == leave)
    def _():
        for k in range(8):
            y_ref[0, :, 0, k, :] = y_scr[k]

    @pl.when(c == nsteps - 1)
    def _():
        sout_ref[0] = s_scr[...]


def _hgrn_scan(u, lb, s0, *, rev, col_major):
    sel, msk = _hgrn_consts(rev)
    s_shape = s0.shape
    const_specs = [pl.BlockSpec((1, HG_F), lambda i, c: (0, 0)),
                   pl.BlockSpec((HG_SEL_ROWS, HG_CHUNK), lambda i, c: (0, 0)),
                   pl.BlockSpec((HG_LEVELS + 1, HG_CHUNK, HG_CHUNK), lambda i, c: (0, 0, 0)),
                   pl.BlockSpec((1,) + s_shape[1:], lambda i, c: (i, 0, 0, 0))]
    s_out = pl.BlockSpec((1,) + s_shape[1:], lambda i, c: (i, 0, 0, 0))
    name = ("hgrn_cols" if col_major else "hgrn_rows") + ("_bwd" if rev else "_fwd")
    cols = (COL_HQ // HG_F, COL_HGF // HG_F + rev, COL_HGI // HG_F)
    if col_major:
        b, r, wg, _, _ = u.shape
        nsteps = wg * 8
        nsub = r // HG_CHUNK
        st = (lambda c: nsteps - 1 - c) if rev else (lambda c: c)
        blk = (1, r, 1, 8, HG_F)
        in_specs = [pl.BlockSpec(blk, lambda i, c, cb=cb: (i, 0, st(c) // 8, 0, cb)) for cb in cols]
        out_specs = [pl.BlockSpec(blk, lambda i, c: (i, 0, st(c) // 8, 0, 0)), s_out]
        out_shape = [jax.ShapeDtypeStruct((b, r, wg, 8, HG_D), F32), jax.ShapeDtypeStruct(s_shape, F32)]
        scratch = [pltpu.VMEM((8, r, HG_F), F32)] * 4 + [pltpu.VMEM(s_shape[1:], F32)]
        body = functools.partial(_hgrn_cols_kernel, nsteps=nsteps, nsub=nsub, rev=rev)
    else:
        b, t, _ = u.shape
        nsub = STEP_CHUNKS
        rows = nsub * HG_CHUNK
        nsteps = t // rows
        st = (lambda c: nsteps - 1 - c) if rev else (lambda c: c)
        in_specs = [pl.BlockSpec((1, rows, HG_F), lambda i, c, cb=cb: (i, st(c), cb)) for cb in cols]
        out_specs = [pl.BlockSpec((1, rows, HG_D), lambda i, c: (i, st(c), 0)), s_out]
        out_shape = [jax.ShapeDtypeStruct((b, t, HG_D), F32), jax.ShapeDtypeStruct(s_shape, F32)]
        scratch = [pltpu.VMEM(s_shape[1:], F32)]
        body = functools.partial(_hgrn_rows_kernel, nsteps=nsteps, nsub=nsub, rev=rev)
    return pl.pallas_call(
        body,
        grid=(b, nsteps),
        in_specs=in_specs + const_specs,
        out_specs=out_specs,
        out_shape=out_shape,
        scratch_shapes=scratch,
        compiler_params=_cparams(2),
        name=name,
    )(u, u, u, lb, sel, msk, s0)


def _pad_rows(a, rows):
    return jnp.concatenate([a, jnp.zeros((rows - a.shape[0],) + a.shape[1:], a.dtype)], axis=0)


def _lane_vec(parts, width=128):
    out = jnp.zeros((width,), F32)
    for off, vals in parts:
        out = lax.dynamic_update_slice(out, vals.astype(F32), (off,))
    return out


def _permute_w_in(w_in):
    offs = np.cumsum([0, SSD_D, SSD_XBC, 2 * SSD_HEADS, 2 * ML_D, ML_D, ML_D, 2 * ML_HEADS, 2 * ML_HEADS,
                      HG_F, 2 * HG_F, HG_D, HG_D])
    z, xbc, dt, mqk, mv, mo, ig, fg, hq, hf, hi, hg = [w_in[:, offs[i]:offs[i + 1]] for i in range(12)]
    small = jnp.concatenate([dt, ig, fg], axis=1)
    small = jnp.concatenate([small, jnp.zeros((w_in.shape[0], 512 - small.shape[1]), w_in.dtype)], axis=1)
    return jnp.concatenate([xbc, hi, z, mqk, hf, mv, mo, hq, hg, small], axis=1).astype(BF16)


def kernel(x, c, ctx, c_ctx, ada_w, ada_b, norm_w, ffn_w_gate, ffn_w_up, ffn_w_down, w_in, w_out,
           ssd_conv_w, ssd_conv_b, ssd_a_log, ssd_dt_bias, ssd_d, ssd_norm_w, ml_conv_w, ml_conv_b,
           ml_ig_b, ml_fg_b, ml_norm_w, hg_lb_logits, hg_norm_w, final_norm_w):
    bsz, seq, d = x.shape
    ctx_len = ctx.shape[1]
    depth = ada_w.shape[0]
    rows = seq // GRID_W
    n_stream = bsz + 1

    pr = jax.nn.softmax(hg_lb_logits.astype(F32), axis=0)
    cum = jnp.cumsum(pr, axis=0)
    lb_all = cum - cum[0]

    cvec = _pad_rows(jnp.concatenate([c, c_ctx[None, :]], axis=0), 8)
    mods = _ada_mods(cvec, ada_w, ada_b)

    xl = x.reshape(bsz * seq, d)
    xc = ctx.reshape(bsz * ctx_len, d)
    tm_l = 512
    tm_c = min(512, bsz * ctx_len)
    lat_stream = lambda tm: (lambda i: i // (seq // tm))
    ctx_stream = lambda tm: (lambda i: bsz)
    fw = final_norm_w.reshape(1, d).astype(F32)
    zeros = lambda shape: jnp.zeros(shape, F32)

    for l in range(depth):
        need_ctx = l < depth - 1
        mod = mods[l, :n_stream].reshape(n_stream, 3, 3, d)

        def pack(sub):
            nw = jnp.broadcast_to(norm_w[l, sub][None, None, :], (n_stream, 1, d))
            return jnp.concatenate([mod[:, sub], nw, zeros((n_stream, 4, d))], axis=1)

        wg = ffn_w_gate[l].astype(BF16)
        wu = ffn_w_up[l].astype(BF16)
        wd = ffn_w_down[l].astype(BF16)

        pk0 = pack(0)
        xl = _ffn(xl, pk0, lat_stream(tm_l), wg[0], wu[0], wd[0], fw, tm=tm_l, tf=512)
        xc = _ffn(xc, pk0, ctx_stream(tm_c), wg[0], wu[0], wd[0], fw, tm=tm_c, tf=512)

        pk1 = pack(1)
        wp = _permute_w_in(w_in[l])
        ul = _proj(xl, pk1, lat_stream(tm_l), wp, tm=tm_l, tn=1536).reshape(bsz, seq, U_COLS)
        uc = _proj(xc, pk1, ctx_stream(tm_c), wp, tm=tm_c, tn=1536).reshape(bsz, ctx_len, U_COLS)

        a_neg = -jnp.exp(ssd_a_log[l].astype(F32))
        s_cw = _pad_rows(ssd_conv_w[l].astype(F32), 8)
        s_cb = ssd_conv_b[l].reshape(1, SSD_XBC).astype(F32)
        dv = jnp.repeat(ssd_d[l].astype(F32), SSD_HEAD_DIM).reshape(1, SSD_D)
        hp = jnp.stack([_pad_rows(jnp.stack([_lane_vec([(SM_DT + SSD_HEADS * r, ssd_dt_bias[l, r])]),
                                             _lane_vec([(SM_DT + SSD_HEADS * r, a_neg[r])])]), 8)
                        for r in (0, 1)])
        h0 = zeros((2, bsz, SSD_HEADS // 2, SSD_STATE, 2 * SSD_HEAD_DIM))
        ysc_f, ysc_b, hc = _ssd_scan(uc, s_cw, s_cb, hp, dv, h0)
        ysl_f, ysl_b, _ = _ssd_scan(ul, s_cw, s_cb, hp, dv, hc)

        m_cw = _pad_rows(ml_conv_w[l].astype(F32), 8)
        m_cb = ml_conv_b[l].reshape(1, 2 * ML_D).astype(F32)
        gb = jnp.stack([_pad_rows(jnp.stack([_lane_vec([(SM_IG + ML_HEADS * r, ml_ig_b[l, r])]),
                                             _lane_vec([(SM_FG + ML_HEADS * r, ml_fg_b[l, r])])]), 8)
                        for r in (0, 1)])
        c0 = zeros((2, bsz, ML_HEADS, ML_HEAD_DIM, 2 * ML_HEAD_DIM))
        m0 = zeros((2, bsz, ML_HEADS, 8, 128))
        ymc_f, ymc_b, cc, mc = _mlstm_scan(uc, m_cw, m_cb, gb, c0, m0)
        yml_f, yml_b, _, _ = _mlstm_scan(ul, m_cw, m_cb, gb, cc, mc)

        lb = lb_all[l].reshape(1, HG_F)
        ul_grid = ul.reshape(bsz, rows, GRID_W // 8, 8, U_COLS)
        yh_l, yh_c = [], []
        for rev in (0, 1):
            s0 = zeros((bsz, HG_HEADS, HG_DV, HG_DK))
            yc_, sc = _hgrn_scan(uc, lb, s0, rev=rev, col_major=False)
            yl_, _ = _hgrn_scan(ul_grid, lb, sc, rev=rev, col_major=True)
            yh_c.append(yc_.reshape(bsz * ctx_len, HG_D))
            yh_l.append(yl_.reshape(bsz * seq, HG_D))

        nw = jnp.concatenate([ssd_norm_w[l], ml_norm_w[l], hg_norm_w[l]]).reshape(1, d).astype(F32)
        wo = w_out[l].astype(BF16)
        flat = lambda a: a.reshape(-1, a.shape[-1])
        xl = _merge(xl, pk1, lat_stream(256), [flat(ysl_f), flat(ysl_b)], [flat(yml_f), flat(yml_b)], yh_l,
                    flat(ul), nw, wo, tm=256)

        pk2 = pack(2)
        xl = _ffn(xl, pk2, lat_stream(tm_l), wg[1], wu[1], wd[1], fw, tm=tm_l, tf=512, final=not need_ctx)
        if need_ctx:
            xc = _merge(xc, pk1, ctx_stream(256), [flat(ysc_f), flat(ysc_b)], [flat(ymc_f), flat(ymc_b)], yh_c,
                        flat(uc), nw, wo, tm=256)
            xc = _ffn(xc, pk2, ctx_stream(tm_c), wg[1], wu[1], wd[1], fw, tm=tm_c, tf=512)

    return xl.reshape(bsz, seq, d)
```

```python
import functools

import numpy as np
import jax
import jax.numpy as jnp
from jax import lax
from jax.experimental import pallas as pl
from jax.experimental.pallas import tpu as pltpu

F32 = jnp.float32
BF16 = jnp.bfloat16

GRID_W = 64
FFN_HALF = 0.5
EPS = 1e-6
CHUNK = 128
HG_CHUNK = 128
CONV_W = 5

SSD_HEADS = 16
SSD_HEAD_DIM = 64
SSD_GROUPS = 2
SSD_STATE = 128
ML_HEADS = 4
ML_HEAD_DIM = 128
HG_HEADS = 4
HG_DK = 128
HG_DV = 128

SSD_D = SSD_HEADS * SSD_HEAD_DIM
SSD_XBC = SSD_D + 2 * SSD_GROUPS * SSD_STATE
ML_D = ML_HEADS * ML_HEAD_DIM
HG_F = HG_HEADS * HG_DK
HG_D = HG_HEADS * HG_DV

COL_MQK = 0
COL_HGI = 1024
COL_XBC = 1536
COL_Z = 3072
COL_HGF = 4096
COL_MV = 5120
COL_MO = 5632
COL_HQ = 6144
COL_HGG = 6656
COL_SM = 7168
U_COLS = 7680
SM_DT = 0
SM_IG = 2 * SSD_HEADS
SM_FG = 2 * SSD_HEADS + 2 * ML_HEADS

NEG = -1e30
VMEM_LIMIT_BYTES = 56 * 1024 * 1024


def _cparams(n_axes):
    return pltpu.CompilerParams(dimension_semantics=("arbitrary",) * n_axes,
                                vmem_limit_bytes=VMEM_LIMIT_BYTES)


def _sigmoid(x):
    return 1.0 / (1.0 + jnp.exp(-x))


def _silu(x):
    return x * _sigmoid(x)


def _log_sigmoid(x):
    return jnp.minimum(x, 0.0) - jnp.log1p(jnp.exp(-jnp.abs(x)))


def _softplus(x):
    return jnp.maximum(x, 0.0) + jnp.log1p(jnp.exp(-jnp.abs(x)))


def _dot(a, b):
    return jnp.dot(a, b, preferred_element_type=F32)


def _dot_nt(a, b):
    return lax.dot_general(a, b, (((1,), (1,)), ((), ())), preferred_element_type=F32)


def _dot_tn(a, b):
    return lax.dot_general(a, b, (((0,), (0,)), ((), ())), preferred_element_type=F32)


def _dot_sel(sel_bf16, x):
    hi = x.astype(BF16)
    r1 = x - hi.astype(F32)
    mid = r1.astype(BF16)
    lo = (r1 - mid.astype(F32)).astype(BF16)
    return _dot(sel_bf16, hi) + _dot(sel_bf16, mid) + _dot(sel_bf16, lo)


def _norm_mod(x, pk):
    ms = jnp.mean(x * x, axis=-1, keepdims=True)
    xn = (x * lax.rsqrt(ms + EPS)) * pk[3:4]
    return xn * (1.0 + pk[1:2]) + pk[0:1]


def _ada_kernel(c_ref, w_ref, b_ref, o_ref):
    c = _silu(c_ref[...]).astype(BF16)
    o_ref[0] = _dot(c, w_ref[0].astype(BF16)) + b_ref[0]


def _ada_mods(cvec, ada_w, ada_b):
    depth, d, n = ada_w.shape
    tn = 1024
    return pl.pallas_call(
        _ada_kernel,
        grid=(depth, n // tn),
        in_specs=[pl.BlockSpec((8, d), lambda l, j: (0, 0)),
                  pl.BlockSpec((1, d, tn), lambda l, j: (l, 0, j)),
                  pl.BlockSpec((1, 1, tn), lambda l, j: (l, 0, j))],
        out_specs=pl.BlockSpec((1, 8, tn), lambda l, j: (l, 0, j)),
        out_shape=jax.ShapeDtypeStruct((depth, 8, n), F32),
        compiler_params=_cparams(2),
        name="ada_mods",
    )(cvec, ada_w, ada_b.reshape(depth, 1, n))


def _ffn_kernel(x_ref, pk_ref, wg_ref, wu_ref, wd_ref, fw_ref, o_ref, h_scr, acc_scr, *, nf, final):
    f = pl.program_id(1)

    @pl.when(f == 0)
    def _():
        h_scr[...] = _norm_mod(x_ref[...], pk_ref[0]).astype(BF16)
        acc_scr[...] = jnp.zeros_like(acc_scr)

    h = h_scr[...]
    g = _dot(h, wg_ref[...])
    u = _dot(h, wu_ref[...])
    a = (_silu(g) * u).astype(BF16)
    acc_scr[...] += _dot(a, wd_ref[...])

    @pl.when(f == nf - 1)
    def _():
        y = x_ref[...] + (FFN_HALF * pk_ref[0][2:3]) * acc_scr[...]
        if final:
            ms = jnp.mean(y * y, axis=-1, keepdims=True)
            y = (y * lax.rsqrt(ms + EPS)) * fw_ref[...]
        o_ref[...] = y


def _ffn(x, pack, stream_of_tile, wg, wu, wd, fw, *, tm, tf, final=False):
    m, d = x.shape
    fdim = wg.shape[1]
    nf = fdim // tf
    return pl.pallas_call(
        functools.partial(_ffn_kernel, nf=nf, final=final),
        grid=(m // tm, nf),
        in_specs=[pl.BlockSpec((tm, d), lambda i, f: (i, 0)),
                  pl.BlockSpec((1, 8, d), lambda i, f: (stream_of_tile(i), 0, 0)),
                  pl.BlockSpec((d, tf), lambda i, f: (0, f)),
                  pl.BlockSpec((d, tf), lambda i, f: (0, f)),
                  pl.BlockSpec((tf, d), lambda i, f: (f, 0)),
                  pl.BlockSpec((1, d), lambda i, f: (0, 0))],
        out_specs=pl.BlockSpec((tm, d), lambda i, f: (i, 0)),
        out_shape=jax.ShapeDtypeStruct((m, d), F32),
        scratch_shapes=[pltpu.VMEM((tm, d), BF16), pltpu.VMEM((tm, d), F32)],
        compiler_params=_cparams(2),
        name="ffn",
    )(x, pack, wg, wu, wd, fw)


HALO = 16


def _conv_silu(raw, cw_ref, cb_ref, rows):
    total, width = raw.shape
    acc = None
    for j in range(CONV_W):
        sh = raw if j == CONV_W // 2 else pltpu.roll(raw, (CONV_W // 2 - j) % total, axis=0)
        w8 = jnp.broadcast_to(cw_ref[j:j + 1, :], (8, width))
        term = sh[HALO:HALO + rows].reshape(rows // 8, 8, width) * w8[None]
        acc = term if acc is None else acc + term
    acc = acc + jnp.broadcast_to(cb_ref[...], (8, width))[None]
    return _silu(acc).reshape(rows, width)


def _proj_kernel(x_ref, xp_ref, xn_ref, pk_ref, w_ref, cwa_ref, cba_ref, cwb_ref, cbb_ref, o_ref, h_scr,
                 *, tm, tiles_per_seq):
    i = pl.program_id(0)
    j = pl.program_id(1)
    tn = o_ref.shape[1]

    @pl.when(j == 0)
    def _():
        pk = pk_ref[0]
        h_scr[0:HALO, :] = _norm_mod(xp_ref[...], pk).astype(BF16)
        h_scr[HALO:HALO + tm, :] = _norm_mod(x_ref[...], pk).astype(BF16)
        h_scr[HALO + tm:, :] = _norm_mod(xn_ref[...], pk).astype(BF16)

    def conv_tile(cw_ref, cb_ref, ncols):
        raw = _dot(h_scr[...], w_ref[...])
        keep_p = (i % tiles_per_seq != 0).astype(F32)
        keep_n = (i % tiles_per_seq != tiles_per_seq - 1).astype(F32)
        rc = jnp.concatenate([raw[:HALO, :ncols] * keep_p, raw[HALO:HALO + tm, :ncols],
                              raw[HALO + tm:, :ncols] * keep_n], axis=0)
        o_ref[:, :ncols] = _conv_silu(rc, cw_ref, cb_ref, tm)
        if ncols < tn:
            o_ref[:, ncols:] = raw[HALO:HALO + tm, ncols:]

    @pl.when(j == COL_MQK // tn)
    def _():
        conv_tile(cwa_ref, cba_ref, 2 * ML_D)

    @pl.when(j == COL_XBC // tn)
    def _():
        conv_tile(cwb_ref, cbb_ref, SSD_XBC)

    @pl.when(j > COL_XBC // tn)
    def _():
        o_ref[...] = _dot(h_scr[HALO:HALO + tm, :], w_ref[...])


def _proj(x, pack, stream_of_tile, w, cwa, cba, cwb, cbb, *, tm, tn, seq):
    m, d = x.shape
    n = w.shape[1]
    per = tm // HALO
    last = m // HALO - 1
    return pl.pallas_call(
        functools.partial(_proj_kernel, tm=tm, tiles_per_seq=seq // tm),
        grid=(m // tm, n // tn),
        in_specs=[pl.BlockSpec((tm, d), lambda i, j: (i, 0)),
                  pl.BlockSpec((HALO, d), lambda i, j: (jnp.maximum(i * per - 1, 0), 0)),
                  pl.BlockSpec((HALO, d), lambda i, j: (jnp.minimum((i + 1) * per, last), 0)),
                  pl.BlockSpec((1, 8, d), lambda i, j: (stream_of_tile(i), 0, 0)),
                  pl.BlockSpec((d, tn), lambda i, j: (0, j)),
                  pl.BlockSpec((8, 2 * ML_D), lambda i, j: (0, 0)),
                  pl.BlockSpec((1, 2 * ML_D), lambda i, j: (0, 0)),
                  pl.BlockSpec((8, SSD_XBC), lambda i, j: (0, 0)),
                  pl.BlockSpec((1, SSD_XBC), lambda i, j: (0, 0))],
        out_specs=pl.BlockSpec((tm, tn), lambda i, j: (i, j)),
        out_shape=jax.ShapeDtypeStruct((m, n), F32),
        scratch_shapes=[pltpu.VMEM((tm + 2 * HALO, d), BF16)],
        compiler_params=_cparams(2),
        name="in_proj",
    )(x, x, x, pack, w, cwa, cba, cwb, cbb)


def _group_norm(y, w, group):
    outs = []
    for g in range(y.shape[1] // group):
        yg = y[:, g * group:(g + 1) * group]
        ms = jnp.mean(yg * yg, axis=-1, keepdims=True)
        outs.append(yg * lax.rsqrt(ms + EPS))
    return jnp.concatenate(outs, axis=1) * w


def _merge_kernel(x_ref, pk_ref, sf_ref, sb_ref, mf_ref, mb_ref, hf_ref, hb_ref, z_ref, mo_ref, hg_ref,
                  nw_ref, wo_ref, o_ref):
    nw = nw_ref[...]
    ys = (sf_ref[...] + sb_ref[...]) * _silu(z_ref[...])
    ys = _group_norm(ys, nw[:, :SSD_D], SSD_D // SSD_GROUPS)
    ym = _group_norm(mf_ref[...] + mb_ref[...], nw[:, SSD_D:SSD_D + ML_D], ML_HEAD_DIM) * _sigmoid(mo_ref[...])
    yh = _group_norm(hf_ref[...] + hb_ref[...], nw[:, SSD_D + ML_D:], HG_DV) * _silu(hg_ref[...])
    y = jnp.concatenate([ys, ym, yh], axis=1).astype(BF16)
    o_ref[...] = x_ref[...] + pk_ref[0][2:3] * _dot(y, wo_ref[...])


def _merge(x, pack, stream_of_tile, ys, ym, yh, u, nw, wo, *, tm):
    m, d = x.shape

    def rows(width, colblk):
        return pl.BlockSpec((tm, width), lambda i: (i, colblk))

    return pl.pallas_call(
        _merge_kernel,
        grid=(m // tm,),
        in_specs=[rows(d, 0),
                  pl.BlockSpec((1, 8, d), lambda i: (stream_of_tile(i), 0, 0)),
                  rows(SSD_D, 0), rows(SSD_D, 0), rows(ML_D, 0), rows(ML_D, 0), rows(HG_D, 0), rows(HG_D, 0),
                  rows(SSD_D, COL_Z // SSD_D), rows(ML_D, COL_MO // ML_D), rows(HG_D, COL_HGG // HG_D),
                  pl.BlockSpec((1, d), lambda i: (0, 0)),
                  pl.BlockSpec((d, d), lambda i: (0, 0))],
        out_specs=rows(d, 0),
        out_shape=jax.ShapeDtypeStruct((m, d), F32),
        compiler_params=_cparams(1),
        name="merge_out",
    )(x, pack, ys[0], ys[1], ym[0], ym[1], yh[0], yh[1], u, u, u, nw, wo)


STEP_CHUNKS = 4


def _scan_tri(length, rev):
    r = lax.broadcasted_iota(jnp.int32, (length, length), 0)
    c = lax.broadcasted_iota(jnp.int32, (length, length), 1)
    return (c >= r) if rev else (c <= r)


def _scan_tri3(length, rev):
    hi = lax.broadcasted_iota(jnp.int32, (length // 8, 8, length), 0)
    lo = lax.broadcasted_iota(jnp.int32, (length // 8, 8, length), 1)
    c = lax.broadcasted_iota(jnp.int32, (length // 8, 8, length), 2)
    r = hi * 8 + lo
    return (c >= r) if rev else (c <= r)


def _tiles(x):
    return x.reshape(x.shape[0] // 8, 8, x.shape[1])


def _row8(x, i):
    return jnp.broadcast_to(x[i:i + 1, :], (8, x.shape[1]))[None]


def _ssd_chunk(xbc, raw, hp, dv_ref, state, y_ref, rs, rev):
    L = CHUNK
    P = SSD_HEAD_DIM
    xs = xbc[:, :SSD_D]
    bm = xbc[:, SSD_D:SSD_D + SSD_GROUPS * SSD_STATE]
    cm = xbc[:, SSD_D + SSD_GROUPS * SSD_STATE:]

    off = SM_DT + SSD_HEADS * rev
    dt = _softplus(raw + hp[0:1, :])
    la = dt * hp[1:2, :]
    a = _dot_sel(_scan_tri(L, rev).astype(BF16), la)
    mask = _scan_tri3(L, rev)
    a_end = a[0:1, :] if rev else a[L - 1:L, :]
    eend = jnp.exp(a_end)
    a_t = a.T
    dt_t = dt.T
    wf_t = (jnp.exp(a_end - a) * dt).T
    lane = lax.broadcasted_iota(jnp.int32, (1, 2 * P), 1)
    first = lane < P
    new_state = []

    for g in range(SSD_GROUPS):
        cg = cm[:, g * SSD_STATE:(g + 1) * SSD_STATE]
        bg = bm[:, g * SSD_STATE:(g + 1) * SSD_STATE]
        cb = _tiles(_dot_nt(cg.astype(BF16), bg.astype(BF16)))
        cg3 = _tiles(cg)
        bg_t = _tiles(bg.T)
        for pp in range(SSD_HEADS // SSD_GROUPS // 2):
            pair = g * (SSD_HEADS // SSD_GROUPS // 2) + pp
            xp = xs[:, pair * 2 * P:(pair + 1) * 2 * P]
            hpair = state[pair]
            lhs, rhs, blhs, xms, ends = [], [], [], [], []
            for k in range(2):
                ln = off + 2 * pair + k
                sel = first if k == 0 else jnp.logical_not(first)
                acol = _tiles(jnp.broadcast_to(a[:, ln:ln + 1], (L, L)))
                dec = jnp.exp(jnp.where(mask, acol - _row8(a_t, ln), NEG))
                lhs += [(cb * dec * _row8(dt_t, ln)).reshape(L, L), (cg3 * jnp.exp(acol)).reshape(L, SSD_STATE)]
                xm = jnp.where(sel, xp, 0.0)
                rhs += [xm, jnp.where(sel, hpair, 0.0)]
                blhs.append((bg_t * _row8(wf_t, ln)).reshape(SSD_STATE, L))
                xms.append(xm)
                ends.append(eend[:, ln:ln + 1])
            xcat = jnp.concatenate(xms, axis=0).astype(BF16)
            y = _dot(jnp.concatenate(lhs, axis=1).astype(BF16), jnp.concatenate(rhs, axis=0).astype(BF16))
            if not rev:
                y = y + dv_ref[:, pair * 2 * P:(pair + 1) * 2 * P] * xp
            y_ref[0, rs, pair * 2 * P:(pair + 1) * 2 * P] = y
            upd = _dot(jnp.concatenate(blhs, axis=1).astype(BF16), xcat)
            new_state.append(hpair * jnp.where(first, ends[0], ends[1]) + upd)
    return new_state


def _ssd_kernel(xf_ref, sf_ref, xb_ref, sb_ref, hp_ref, dv_ref, h0_ref, yf_ref, yb_ref, hout_ref, h_scr,
                *, nsteps, nsub):
    c = pl.program_id(1)

    @pl.when(c == 0)
    def _():
        h_scr[...] = h0_ref[:, 0]

    npair = SSD_HEADS // 2
    states = [[h_scr[d, p] for p in range(npair)] for d in range(2)]
    streams = ((xf_ref, sf_ref, yf_ref), (xb_ref, sb_ref, yb_ref))
    for d, (x_ref, s_ref, y_ref) in enumerate(streams):
        subs = range(nsub - 1, -1, -1) if d else range(nsub)
        for u in subs:
            rs = slice(u * CHUNK, (u + 1) * CHUNK)
            states[d] = _ssd_chunk(x_ref[0, rs, :], s_ref[0, rs, 0:128], hp_ref[d], dv_ref, states[d], y_ref, rs, d)
    for d in range(2):
        for p in range(npair):
            h_scr[d, p] = states[d][p]

    @pl.when(c == nsteps - 1)
    def _():
        hout_ref[:, 0] = h_scr[...]


def _ssd_scan(u3, hp, dv, h0):
    b, t, _ = u3.shape
    nsub = min(STEP_CHUNKS, t // CHUNK)
    rows = nsub * CHUNK
    nsteps = t // rows
    fwd = lambda c: c
    bwd = lambda c: nsteps - 1 - c
    st_shape = h0.shape

    def stream_specs(step_of):
        return [pl.BlockSpec((1, rows, SSD_XBC), lambda i, c: (i, step_of(c), COL_XBC // SSD_XBC)),
                pl.BlockSpec((1, rows, 512), lambda i, c: (i, step_of(c), COL_SM // 512))]

    st_spec = pl.BlockSpec((2, 1) + st_shape[2:], lambda i, c: (0, i, 0, 0, 0))
    return pl.pallas_call(
        functools.partial(_ssd_kernel, nsteps=nsteps, nsub=nsub),
        grid=(b, nsteps),
        in_specs=stream_specs(fwd) + stream_specs(bwd) + [
            pl.BlockSpec((2, 8, 128), lambda i, c: (0, 0, 0)),
            pl.BlockSpec((1, SSD_D), lambda i, c: (0, 0)),
            st_spec],
        out_specs=[pl.BlockSpec((1, rows, SSD_D), lambda i, c: (i, fwd(c), 0)),
                   pl.BlockSpec((1, rows, SSD_D), lambda i, c: (i, bwd(c), 0)),
                   st_spec],
        out_shape=[jax.ShapeDtypeStruct((b, t, SSD_D), F32), jax.ShapeDtypeStruct((b, t, SSD_D), F32),
                   jax.ShapeDtypeStruct(st_shape, F32)],
        scratch_shapes=[pltpu.VMEM((2,) + st_shape[2:], F32)],
        compiler_params=_cparams(2),
        name="ssd_scan",
    )(u3, u3, u3, u3, hp, dv, h0)


def _mlstm_chunk(qk, v, raw, gb, state, y_ref, rs, rev):
    L = CHUNK
    K = ML_HEAD_DIM
    logi = raw + gb[0:1, :]
    logf = _log_sigmoid(raw + gb[1:2, :])
    bcum = _dot_sel(_scan_tri(L, rev).astype(BF16), logf)
    mask = _scan_tri3(L, rev)
    b_t = bcum.T
    logi_t = logi.T
    ones = jnp.ones((L, K), BF16)
    scale = K ** -0.5
    end = 0 if rev else L - 1
    new_state = []

    for h in range(ML_HEADS):
        li = SM_IG + ML_HEADS * rev + h
        lf = SM_FG + ML_HEADS * rev + h
        qh = qk[:, h * K:(h + 1) * K]
        kh = qk[:, ML_D + h * K:ML_D + (h + 1) * K] * scale
        vaug = jnp.concatenate([v[:, h * K:(h + 1) * K].astype(BF16), ones], axis=1)
        c_old, m_old = state[h]
        brow = b_t[lf:lf + 1, :]
        irow = logi_t[li:li + 1, :]
        bcol = _tiles(jnp.broadcast_to(bcum[:, lf:lf + 1], (L, L)))
        dm = jnp.where(mask, bcol - jnp.broadcast_to(brow - irow, (8, L))[None], NEG)
        gfull = bcol + m_old
        mt = jnp.maximum(gfull, jnp.max(dm, axis=2, keepdims=True))
        qkm = _tiles(_dot_nt(qh.astype(BF16), kh.astype(BF16))) * jnp.exp(dm - mt)
        qin = _tiles(qh) * jnp.exp(gfull - mt)
        lhs = jnp.concatenate([qkm.reshape(L, L), qin.reshape(L, K)], axis=1).astype(BF16)
        rhs = jnp.concatenate([vaug, c_old.astype(BF16)], axis=0)
        s = _dot(lhs, rhs)
        den = jnp.maximum(jnp.abs(s[:, K:]), jnp.exp(-mt).reshape(L, L))
        y_ref[0, rs, h * K:(h + 1) * K] = s[:, :K] / den
        bl = brow[:, end:end + 1]
        ds = bl - brow + irow
        m_new = jnp.maximum(bl + m_old, jnp.max(ds, axis=1, keepdims=True))
        ws = jnp.exp(ds - m_new)
        keep = jnp.exp(bl + m_old - m_new)
        new_state.append((c_old * keep + _dot((kh.T * ws).astype(BF16), vaug), m_new))
    return new_state


def _mlstm_kernel(qf_ref, vf_ref, sf_ref, qb_ref, vb_ref, sb_ref, gb_ref, c0_ref, m0_ref,
                  yf_ref, yb_ref, cout_ref, mout_ref, c_scr, m_scr, *, nsteps, nsub):
    c = pl.program_id(1)

    @pl.when(c == 0)
    def _():
        c_scr[...] = c0_ref[:, 0]
        m_scr[...] = m0_ref[:, 0]

    states = [[(c_scr[d, h], m_scr[d, h][0:1, 0:1]) for h in range(ML_HEADS)] for d in range(2)]
    streams = ((qf_ref, vf_ref, sf_ref, yf_ref), (qb_ref, vb_ref, sb_ref, yb_ref))
    for d, (q_ref, v_ref, s_ref, y_ref) in enumerate(streams):
        subs = range(nsub - 1, -1, -1) if d else range(nsub)
        for u in subs:
            rs = slice(u * CHUNK, (u + 1) * CHUNK)
            states[d] = _mlstm_chunk(q_ref[0, rs, :], v_ref[0, rs, :], s_ref[0, rs, 0:128], gb_ref[d], states[d],
                                     y_ref, rs, d)
    for d in range(2):
        for h in range(ML_HEADS):
            c_scr[d, h] = states[d][h][0]
            m_scr[d, h] = jnp.broadcast_to(states[d][h][1], (8, 128))

    @pl.when(c == nsteps - 1)
    def _():
        cout_ref[:, 0] = c_scr[...]
        mout_ref[:, 0] = m_scr[...]


def _mlstm_scan(u3, gb, c0, m0):
    b, t, _ = u3.shape
    nsub = min(STEP_CHUNKS, t // CHUNK)
    rows = nsub * CHUNK
    nsteps = t // rows
    fwd = lambda c: c
    bwd = lambda c: nsteps - 1 - c

    def stream_specs(step_of):
        return [pl.BlockSpec((1, rows, 2 * ML_D), lambda i, c: (i, step_of(c), COL_MQK // (2 * ML_D))),
                pl.BlockSpec((1, rows, ML_D), lambda i, c: (i, step_of(c), COL_MV // ML_D)),
                pl.BlockSpec((1, rows, 512), lambda i, c: (i, step_of(c), COL_SM // 512))]

    c_spec = pl.BlockSpec((2, 1) + c0.shape[2:], lambda i, c: (0, i, 0, 0, 0))
    m_spec = pl.BlockSpec((2, 1) + m0.shape[2:], lambda i, c: (0, i, 0, 0, 0))
    return pl.pallas_call(
        functools.partial(_mlstm_kernel, nsteps=nsteps, nsub=nsub),
        grid=(b, nsteps),
        in_specs=stream_specs(fwd) + stream_specs(bwd) + [
            pl.BlockSpec((2, 8, 128), lambda i, c: (0, 0, 0)),
            c_spec, m_spec],
        out_specs=[pl.BlockSpec((1, rows, ML_D), lambda i, c: (i, fwd(c), 0)),
                   pl.BlockSpec((1, rows, ML_D), lambda i, c: (i, bwd(c), 0)),
                   c_spec, m_spec],
        out_shape=[jax.ShapeDtypeStruct((b, t, ML_D), F32), jax.ShapeDtypeStruct((b, t, ML_D), F32),
                   jax.ShapeDtypeStruct(c0.shape, F32), jax.ShapeDtypeStruct(m0.shape, F32)],
        scratch_shapes=[pltpu.VMEM((2,) + c0.shape[2:], F32), pltpu.VMEM((2,) + m0.shape[2:], F32)],
        compiler_params=_cparams(2),
        name="mlstm_scan",
    )(u3, u3, u3, u3, u3, u3, gb, c0, m0)


HG_LEVELS = 7
HG_SEL_ROWS = (2 + HG_LEVELS) * HG_CHUNK


def _hgrn_consts(rev):
    L = HG_CHUNK
    pos = np.arange(L)
    p = pos[::-1] if rev else pos
    cum = (p[None, :] <= p[:, None])
    after = (p[None, :] > p[:, None])
    sels = [cum, after]
    masks = []
    for lev in range(HG_LEVELS):
        m = 1 << lev
        blk = p // (2 * m)
        late = (p % (2 * m)) >= m
        ref = blk * 2 * m + m - 1
        pj = p[None, :]
        sel = np.where(late[:, None], (pj > ref[:, None]) & (pj <= p[:, None]),
                       (pj > p[:, None]) & (pj <= ref[:, None]))
        sels.append(sel)
        masks.append((blk[:, None] == blk[None, :]) & late[:, None] & (~late)[None, :])
    masks.append(np.eye(L, dtype=bool))
    sel_all = np.concatenate(sels, axis=0).astype(np.float32)
    mask_all = np.stack(masks, axis=0).astype(np.float32)
    return jnp.asarray(sel_all, BF16), jnp.asarray(mask_all, F32)


def _hgrn_chunk(q, fr, v, lb, sel_ref, msk_ref, s_ref):
    L = HG_CHUNK
    K = HG_DK
    x1 = jnp.log(lb)
    x2 = jnp.log1p(-lb) + _log_sigmoid(fr)
    logf = jnp.maximum(x1, x2) + jnp.log1p(jnp.exp(-jnp.abs(x1 - x2)))
    k = (1.0 - lb) * _sigmoid(-fr)
    ex = _dot_sel(sel_ref[...], logf)
    bcum = ex[0:L]
    after = ex[L:2 * L]
    q_in = (q * jnp.exp(bcum)).astype(BF16)
    k_w = (k * jnp.exp(after)).astype(BF16)
    e_end = jnp.exp(bcum[0:1] + after[0:1])
    qb = q.astype(BF16)
    kb = k.astype(BF16)
    vb = v.astype(BF16)
    qs, ks = [], []
    for lev in range(HG_LEVELS):
        e = jnp.exp(ex[(2 + lev) * L:(3 + lev) * L])
        qs.append((q * e).astype(BF16))
        ks.append((k * e).astype(BF16))
    outs = []
    for h in range(HG_HEADS):
        sl = slice(h * K, (h + 1) * K)
        att = msk_ref[HG_LEVELS] * _dot_nt(qb[:, sl], kb[:, sl])
        for lev in range(HG_LEVELS):
            att = att + msk_ref[lev] * _dot_nt(qs[lev][:, sl], ks[lev][:, sl])
        st = s_ref[h]
        outs.append(_dot(att.astype(BF16), vb[:, sl]) + _dot_nt(q_in[:, sl], st.astype(BF16)))
        s_ref[h] = st * e_end[:, sl] + _dot_tn(vb[:, sl], k_w[:, sl])
    return jnp.concatenate(outs, axis=1)


def _hgrn_rows_kernel(q_ref, f_ref, v_ref, lb_ref, sel_ref, msk_ref, s0_ref, y_ref, sout_ref, s_scr,
                      *, nsteps, nsub, rev):
    c = pl.program_id(1)

    @pl.when(c == 0)
    def _():
        s_scr[...] = s0_ref[0]

    lb = lb_ref[...]
    for u in (range(nsub - 1, -1, -1) if rev else range(nsub)):
        rs = slice(u * HG_CHUNK, (u + 1) * HG_CHUNK)
        y_ref[0, rs, :] = _hgrn_chunk(q_ref[0, rs, :], f_ref[0, rs, :], v_ref[0, rs, :], lb, sel_ref, msk_ref, s_scr)

    @pl.when(c == nsteps - 1)
    def _():
        sout_ref[0] = s_scr[...]


def _hgrn_cols_kernel(q_ref, f_ref, v_ref, lb_ref, sel_ref, msk_ref, s0_ref, y_ref, sout_ref,
                      q_scr, f_scr, v_scr, y_scr, s_scr, *, nsteps, nsub, rev):
    c = pl.program_id(1)
    w = (nsteps - 1 - c) if rev else c
    wi = w % 8
    enter, leave = (7, 0) if rev else (0, 7)

    @pl.when(c == 0)
    def _():
        s_scr[...] = s0_ref[0]

    @pl.when(wi == enter)
    def _():
        for k in range(8):
            q_scr[k] = q_ref[0, :, 0, k, :]
            f_scr[k] = f_ref[0, :, 0, k, :]
            v_scr[k] = v_ref[0, :, 0, k, :]

    lb = lb_ref[...]
    for u in (range(nsub - 1, -1, -1) if rev else range(nsub)):
        rs = pl.ds(u * HG_CHUNK, HG_CHUNK)
        y_scr[wi, rs, :] = _hgrn_chunk(q_scr[wi, rs, :], f_scr[wi, rs, :], v_scr[wi, rs, :], lb, sel_ref, msk_ref,
                                       s_scr)

    @pl.when(wi == leave)
    def _():
        for k in range(8):
            y_ref[0, :, 0, k, :] = y_scr[k]

    @pl.when(c == nsteps - 1)
    def _():
        sout_ref[0] = s_scr[...]


def _hgrn_scan(u, lb, s0, *, rev, col_major):
    sel, msk = _hgrn_consts(rev)
    s_shape = s0.shape
    const_specs = [pl.BlockSpec((1, HG_F), lambda i, c: (0, 0)),
                   pl.BlockSpec((HG_SEL_ROWS, HG_CHUNK), lambda i, c: (0, 0)),
                   pl.BlockSpec((HG_LEVELS + 1, HG_CHUNK, HG_CHUNK), lambda i, c: (0, 0, 0)),
                   pl.BlockSpec((1,) + s_shape[1:], lambda i, c: (i, 0, 0, 0))]
    s_out = pl.BlockSpec((1,) + s_shape[1:], lambda i, c: (i, 0, 0, 0))
    name = ("hgrn_cols" if col_major else "hgrn_rows") + ("_bwd" if rev else "_fwd")
    cols = (COL_HQ // HG_F, COL_HGF // HG_F + rev, COL_HGI // HG_F)
    if col_major:
        b, r, wg, _, _ = u.shape
        nsteps = wg * 8
        nsub = r // HG_CHUNK
        st = (lambda c: nsteps - 1 - c) if rev else (lambda c: c)
        blk = (1, r, 1, 8, HG_F)
        in_specs = [pl.BlockSpec(blk, lambda i, c, cb=cb: (i, 0, st(c) // 8, 0, cb)) for cb in cols]
        out_specs = [pl.BlockSpec(blk, lambda i, c: (i, 0, st(c) // 8, 0, 0)), s_out]
        out_shape = [jax.ShapeDtypeStruct((b, r, wg, 8, HG_D), F32), jax.ShapeDtypeStruct(s_shape, F32)]
        scratch = [pltpu.VMEM((8, r, HG_F), F32)] * 4 + [pltpu.VMEM(s_shape[1:], F32)]
        body = functools.partial(_hgrn_cols_kernel, nsteps=nsteps, nsub=nsub, rev=rev)
    else:
        b, t, _ = u.shape
        nsub = min(STEP_CHUNKS, t // HG_CHUNK)
        rows = nsub * HG_CHUNK
        nsteps = t // rows
        st = (lambda c: nsteps - 1 - c) if rev else (lambda c: c)
        in_specs = [pl.BlockSpec((1, rows, HG_F), lambda i, c, cb=cb: (i, st(c), cb)) for cb in cols]
        out_specs = [pl.BlockSpec((1, rows, HG_D), lambda i, c: (i, st(c), 0)), s_out]
        out_shape = [jax.ShapeDtypeStruct((b, t, HG_D), F32), jax.ShapeDtypeStruct(s_shape, F32)]
        scratch = [pltpu.VMEM(s_shape[1:], F32)]
        body = functools.partial(_hgrn_rows_kernel, nsteps=nsteps, nsub=nsub, rev=rev)
    return pl.pallas_call(
        body,
        grid=(b, nsteps),
        in_specs=in_specs + const_specs,
        out_specs=out_specs,
        out_shape=out_shape,
        scratch_shapes=scratch,
        compiler_params=_cparams(2),
        name=name,
    )(u, u, u, lb, sel, msk, s0)


def _pad_rows(a, rows):
    return jnp.concatenate([a, jnp.zeros((rows - a.shape[0],) + a.shape[1:], a.dtype)], axis=0)


def _lane_vec(parts, width=128):
    out = jnp.zeros((width,), F32)
    for off, vals in parts:
        out = lax.dynamic_update_slice(out, vals.astype(F32), (off,))
    return out


def _permute_w_in(w_in):
    offs = np.cumsum([0, SSD_D, SSD_XBC, 2 * SSD_HEADS, 2 * ML_D, ML_D, ML_D, 2 * ML_HEADS, 2 * ML_HEADS,
                      HG_F, 2 * HG_F, HG_D, HG_D])
    z, xbc, dt, mqk, mv, mo, ig, fg, hq, hf, hi, hg = [w_in[:, offs[i]:offs[i + 1]] for i in range(12)]
    pad = jnp.zeros((w_in.shape[0], 512 - (dt.shape[1] + ig.shape[1] + fg.shape[1])), w_in.dtype)
    parts = [mqk, hi, xbc, z, hf, mv, mo, hq, hg, dt, ig, fg, pad]
    return jnp.concatenate([p.astype(BF16) for p in parts], axis=1)


def kernel(x, c, ctx, c_ctx, ada_w, ada_b, norm_w, ffn_w_gate, ffn_w_up, ffn_w_down, w_in, w_out,
           ssd_conv_w, ssd_conv_b, ssd_a_log, ssd_dt_bias, ssd_d, ssd_norm_w, ml_conv_w, ml_conv_b,
           ml_ig_b, ml_fg_b, ml_norm_w, hg_lb_logits, hg_norm_w, final_norm_w):
    bsz, seq, d = x.shape
    ctx_len = ctx.shape[1]
    depth = ada_w.shape[0]
    rows = seq // GRID_W
    n_stream = bsz + 1

    pr = jax.nn.softmax(hg_lb_logits.astype(F32), axis=0)
    cum = jnp.cumsum(pr, axis=0)
    lb_all = cum - cum[0]

    cvec = _pad_rows(jnp.concatenate([c, c_ctx[None, :]], axis=0), 8)
    mods = _ada_mods(cvec, ada_w, ada_b)

    xl = x.reshape(bsz * seq, d)
    xc = ctx.reshape(bsz * ctx_len, d)
    tm_l = 512
    tm_c = min(512, bsz * ctx_len)
    lat_stream = lambda tm: (lambda i: i // (seq // tm))
    ctx_stream = lambda tm: (lambda i: bsz)
    fw = final_norm_w.reshape(1, d).astype(F32)
    zeros = lambda shape: jnp.zeros(shape, F32)

    for l in range(depth):
        need_ctx = l < depth - 1
        mod = mods[l, :n_stream].reshape(n_stream, 3, 3, d)

        def pack(sub):
            nw = jnp.broadcast_to(norm_w[l, sub][None, None, :], (n_stream, 1, d))
            return jnp.concatenate([mod[:, sub], nw, zeros((n_stream, 4, d))], axis=1)

        wg = [ffn_w_gate[l, s].astype(BF16) for s in (0, 1)]
        wu = [ffn_w_up[l, s].astype(BF16) for s in (0, 1)]
        wd = [ffn_w_down[l, s].astype(BF16) for s in (0, 1)]

        pk0 = pack(0)
        xl = _ffn(xl, pk0, lat_stream(tm_l), wg[0], wu[0], wd[0], fw, tm=tm_l, tf=512)
        xc = _ffn(xc, pk0, ctx_stream(tm_c), wg[0], wu[0], wd[0], fw, tm=tm_c, tf=512)

        pk1 = pack(1)
        wp = _permute_w_in(w_in[l])
        s_cw = _pad_rows(ssd_conv_w[l].astype(F32), 8)
        s_cb = ssd_conv_b[l].reshape(1, SSD_XBC).astype(F32)
        m_cw = _pad_rows(ml_conv_w[l].astype(F32), 8)
        m_cb = ml_conv_b[l].reshape(1, 2 * ML_D).astype(F32)
        ul = _proj(xl, pk1, lat_stream(tm_l), wp, m_cw, m_cb, s_cw, s_cb, tm=tm_l, tn=1536, seq=seq)
        uc = _proj(xc, pk1, ctx_stream(ctx_len), wp, m_cw, m_cb, s_cw, s_cb, tm=ctx_len, tn=1536, seq=ctx_len)
        ul = ul.reshape(bsz, seq, U_COLS)
        uc = uc.reshape(bsz, ctx_len, U_COLS)

        a_neg = -jnp.exp(ssd_a_log[l].astype(F32))
        dv = jnp.repeat(ssd_d[l].astype(F32), SSD_HEAD_DIM).reshape(1, SSD_D)
        hp = jnp.stack([_pad_rows(jnp.stack([_lane_vec([(SM_DT + SSD_HEADS * r, ssd_dt_bias[l, r])]),
                                             _lane_vec([(SM_DT + SSD_HEADS * r, a_neg[r])])]), 8)
                        for r in (0, 1)])
        h0 = zeros((2, bsz, SSD_HEADS // 2, SSD_STATE, 2 * SSD_HEAD_DIM))
        ysc_f, ysc_b, hc = _ssd_scan(uc, hp, dv, h0)
        ysl_f, ysl_b, _ = _ssd_scan(ul, hp, dv, hc)

        gb =jnp.stack([_pad_rows(jnp.stack([_lane_vec([(SM_IG + ML_HEADS * r, ml_ig_b[l, r])]),
                                             _lane_vec([(SM_FG + ML_HEADS * r, ml_fg_b[l, r])])]), 8)
                        for r in (0, 1)])
        c0 = zeros((2, bsz, ML_HEADS, ML_HEAD_DIM, 2 * ML_HEAD_DIM))
        m0 = zeros((2, bsz, ML_HEADS, 8, 128))
        ymc_f, ymc_b, cc, mc = _mlstm_scan(uc, gb, c0, m0)
        yml_f, yml_b, _, _ = _mlstm_scan(ul, gb, cc, mc)

        lb = lb_all[l].reshape(1, HG_F)
        ul_grid = ul.reshape(bsz, rows, GRID_W // 8, 8, U_COLS)
        yh_l, yh_c = [], []
        for rev in (0, 1):
            s0 = zeros((bsz, HG_HEADS, HG_DV, HG_DK))
            yc_, sc = _hgrn_scan(uc, lb, s0, rev=rev, col_major=False)
            yl_, _ = _hgrn_scan(ul_grid, lb, sc, rev=rev, col_major=True)
            yh_c.append(yc_.reshape(bsz * ctx_len, HG_D))
            yh_l.append(yl_.reshape(bsz * seq, HG_D))

        nw = jnp.concatenate([ssd_norm_w[l], ml_norm_w[l], hg_norm_w[l]]).reshape(1, d).astype(F32)
        wo = w_out[l].astype(BF16)
        flat = lambda a: a.reshape(-1, a.shape[-1])
        xl = _merge(xl, pk1, lat_stream(256), [flat(ysl_f), flat(ysl_b)], [flat(yml_f), flat(yml_b)], yh_l,
                    flat(ul), nw, wo, tm=256)

        pk2 = pack(2)
        xl = _ffn(xl, pk2, lat_stream(tm_l), wg[1], wu[1], wd[1], fw, tm=tm_l, tf=512, final=not need_ctx)
        if need_ctx:
            xc = _merge(xc, pk1, ctx_stream(256), [flat(ysc_f), flat(ysc_b)], [flat(ymc_f), flat(ymc_b)], yh_c,
                        flat(uc), nw, wo, tm=256)
            xc = _ffn(xc, pk2, ctx_stream(tm_c), wg[1], wu[1], wd[1], fw, tm=tm_c, tf=512)

    return xl.reshape(bsz, seq, d)
```

```python
import functools

import numpy as np
import jax
import jax.numpy as jnp
from jax import lax
from jax.experimental import pallas as pl
from jax.experimental.pallas import tpu as pltpu

F32 = jnp.float32
BF16 = jnp.bfloat16

GRID_W = 64
FFN_HALF = 0.5
EPS = 1e-6
CHUNK = 128
HG_CHUNK = 128
CONV_W = 5

SSD_HEADS = 16
SSD_HEAD_DIM = 64
SSD_GROUPS = 2
SSD_STATE = 128
ML_HEADS = 4
ML_HEAD_DIM = 128
HG_HEADS = 4
HG_DK = 128
HG_DV = 128

SSD_D = SSD_HEADS * SSD_HEAD_DIM
SSD_XBC = SSD_D + 2 * SSD_GROUPS * SSD_STATE
ML_D = ML_HEADS * ML_HEAD_DIM
HG_F = HG_HEADS * HG_DK
HG_D = HG_HEADS * HG_DV

COL_MQK = 0
COL_HGI = 1024
COL_XBC = 1536
COL_Z = 3072
COL_HGF = 4096
COL_MV = 5120
COL_MO = 5632
COL_HQ = 6144
COL_HGG = 6656
COL_SM = 7168
U_COLS = 7680
UC_COLS = 3072
SM_DT = 0
SM_IG = 2 * SSD_HEADS
SM_FG = 2 * SSD_HEADS + 2 * ML_HEADS

NEG = -1e30
VMEM_LIMIT_BYTES = 56 * 1024 * 1024


def _cparams(n_axes):
    return pltpu.CompilerParams(dimension_semantics=("arbitrary",) * n_axes,
                                vmem_limit_bytes=VMEM_LIMIT_BYTES)


def _sigmoid(x):
    return 1.0 / (1.0 + jnp.exp(-x))


def _silu(x):
    return x * _sigmoid(x)


def _log1p_exp_neg(y):
    return jnp.log(1.0 + jnp.exp(-y))


def _log_sigmoid(x):
    return jnp.minimum(x, 0.0) - _log1p_exp_neg(jnp.abs(x))


def _softplus(x):
    return jnp.maximum(x, 0.0) + jnp.log1p(jnp.exp(-jnp.abs(x)))


def _dot(a, b):
    return jnp.dot(a, b, preferred_element_type=F32)


def _dot_nt(a, b):
    return lax.dot_general(a, b, (((1,), (1,)), ((), ())), preferred_element_type=F32)


def _dot_tn(a, b):
    return lax.dot_general(a, b, (((0,), (0,)), ((), ())), preferred_element_type=F32)


def _dot_sel(sel_bf16, x):
    hi = x.astype(BF16)
    r1 = x - hi.astype(F32)
    mid = r1.astype(BF16)
    lo = (r1 - mid.astype(F32)).astype(BF16)
    return _dot(sel_bf16, hi) + _dot(sel_bf16, mid) + _dot(sel_bf16, lo)


def _norm_mod(x, pk):
    ms = jnp.mean(x * x, axis=-1, keepdims=True)
    xn = (x * lax.rsqrt(ms + EPS)) * pk[3:4]
    return xn * (1.0 + pk[1:2]) + pk[0:1]


def _ada_kernel(c_ref, w_ref, b_ref, o_ref):
    c = _silu(c_ref[...]).astype(BF16)
    o_ref[0] = _dot(c, w_ref[0].astype(BF16)) + b_ref[0]


def _ada_mods(cvec, ada_w, ada_b):
    depth, d, n = ada_w.shape
    tn = 1024
    return pl.pallas_call(
        _ada_kernel,
        grid=(depth, n // tn),
        in_specs=[pl.BlockSpec((8, d), lambda l, j: (0, 0)),
                  pl.BlockSpec((1, d, tn), lambda l, j: (l, 0, j)),
                  pl.BlockSpec((1, 1, tn), lambda l, j: (l, 0, j))],
        out_specs=pl.BlockSpec((1, 8, tn), lambda l, j: (l, 0, j)),
        out_shape=jax.ShapeDtypeStruct((depth, 8, n), F32),
        compiler_params=_cparams(2),
        name="ada_mods",
    )(cvec, ada_w, ada_b.reshape(depth, 1, n))


def _ffn_kernel(x_ref, pk_ref, wg_ref, wu_ref, wd_ref, fw_ref, o_ref, h_scr, a_scr, acc_scr, *, nf, final):
    f = pl.program_id(1)

    def gate_up():
        h = h_scr[...]
        g = _dot(h, wg_ref[...])
        u = _dot(h, wu_ref[...])
        return (_silu(g) * u).astype(BF16)

    @pl.when(f == 0)
    def _():
        h_scr[...] = _norm_mod(x_ref[...], pk_ref[0]).astype(BF16)
        a_scr[...] = gate_up()

    @pl.when(f == 1)
    def _():
        a_new = gate_up()
        acc_scr[...] = _dot(a_scr[...], wd_ref[...])
        a_scr[...] = a_new

    @pl.when(jnp.logical_and(f > 1, f < nf))
    def _():
        a_new = gate_up()
        acc_scr[...] += _dot(a_scr[...], wd_ref[...])
        a_scr[...] = a_new

    @pl.when(f == nf)
    def _():
        acc = acc_scr[...] + _dot(a_scr[...], wd_ref[...])
        y = x_ref[...] + (FFN_HALF * pk_ref[0][2:3]) * acc
        if final:
            ms = jnp.mean(y * y, axis=-1, keepdims=True)
            y = (y * lax.rsqrt(ms + EPS)) * fw_ref[...]
        o_ref[...] = y


def _ffn(x, pack, stream_of_tile, wg, wu, wd, fw, *, tm, tf, final=False):
    m, d = x.shape
    fdim = wg.shape[1]
    nf = fdim // tf
    assert nf >= 2
    cur = lambda f: jnp.minimum(f, nf - 1)
    prev = lambda f: jnp.maximum(f - 1, 0)
    return pl.pallas_call(
        functools.partial(_ffn_kernel, nf=nf, final=final),
        grid=(m // tm, nf + 1),
        in_specs=[pl.BlockSpec((tm, d), lambda i, f: (i, 0)),
                  pl.BlockSpec((1, 8, d), lambda i, f: (stream_of_tile(i), 0, 0)),
                  pl.BlockSpec((d, tf), lambda i, f: (0, cur(f))),
                  pl.BlockSpec((d, tf), lambda i, f: (0, cur(f))),
                  pl.BlockSpec((tf, d), lambda i, f: (prev(f), 0)),
                  pl.BlockSpec((1, d), lambda i, f: (0, 0))],
        out_specs=pl.BlockSpec((tm, d), lambda i, f: (i, 0)),
        out_shape=jax.ShapeDtypeStruct((m, d), F32),
        scratch_shapes=[pltpu.VMEM((tm, d), BF16), pltpu.VMEM((tm, tf), BF16), pltpu.VMEM((tm, d), F32)],
        compiler_params=_cparams(2),
        name="ffn",
    )(x, pack, wg, wu, wd, fw)


HALO = 16


def _proj_kernel(x_ref, xp_ref, xn_ref, pk_ref, w_ref, cwa_ref, cba_ref, cwb_ref, cbb_ref, oc_ref, o_ref,
                 h_scr, raw_scr, *, tm, tiles_per_seq):
    i = pl.program_id(0)
    j = pl.program_id(1)
    tn = o_ref.shape[1]

    def conv_tile(slot, cw_ref, cb_ref, ncols):
        keep_p = (i % tiles_per_seq != 0).astype(F32)
        keep_n = (i % tiles_per_seq != tiles_per_seq - 1).astype(F32)
        rb, cb = 64, 256
        for c0 in range(0, ncols, cb):
            w8 = [jnp.broadcast_to(cw_ref[t:t + 1, c0:c0 + cb], (8, cb))[None] for t in range(CONV_W)]
            b8 = jnp.broadcast_to(cb_ref[:, c0:c0 + cb], (8, cb))[None]
            for r0 in range(0, tm, rb):
                blk = raw_scr[slot, HALO + r0 - 8:HALO + r0 + rb + 8, c0:c0 + cb]
                if r0 == 0:
                    blk = jnp.concatenate([blk[:8] * keep_p, blk[8:]], axis=0)
                if r0 == tm - rb:
                    blk = jnp.concatenate([blk[:rb + 8], blk[rb + 8:] * keep_n], axis=0)
                acc = b8
                for t in range(CONV_W):
                    sh = blk if t == CONV_W // 2 else pltpu.roll(blk, (CONV_W // 2 - t) % (rb + 16), axis=0)
                    acc = acc + _tiles(sh[8:8 + rb]) * w8[t]
                oc_ref[r0:r0 + rb, c0:c0 + cb] = _silu(acc).reshape(rb, cb)
        if ncols < tn:
            oc_ref[:, ncols:] = raw_scr[slot, HALO:HALO + tm, ncols:]

    @pl.when(j == 0)
    def _():
        pk = pk_ref[0]
        h_scr[0:HALO, :] = _norm_mod(xp_ref[...], pk).astype(BF16)
        h_scr[HALO:HALO + tm, :] = _norm_mod(x_ref[...], pk).astype(BF16)
        h_scr[HALO + tm:, :] = _norm_mod(xn_ref[...], pk).astype(BF16)
        raw_scr[0] = _dot(h_scr[...], w_ref[...])

    @pl.when(j == 1)
    def _():
        conv_tile(0, cwa_ref, cba_ref, 2 * ML_D)
        raw_scr[1] = _dot(h_scr[...], w_ref[...])

    @pl.when(j == 2)
    def _():
        conv_tile(1, cwb_ref, cbb_ref, SSD_XBC)
        o_ref[...] = _dot(h_scr[HALO:HALO + tm, :], w_ref[...])

    @pl.when(j > 2)
    def _():
        o_ref[...] = _dot(h_scr[HALO:HALO + tm, :], w_ref[...])


def _proj(x, pack, stream_of_tile, w, cwa, cba, cwb, cbb, *, tm, tn, seq):
    m, d = x.shape
    assert (COL_MQK, COL_XBC, UC_COLS) == (0, tn, 2 * tn)
    per = tm // HALO
    last = m // HALO - 1
    nconv = UC_COLS // tn
    return pl.pallas_call(
        functools.partial(_proj_kernel, tm=tm, tiles_per_seq=seq // tm),
        grid=(m // tm, U_COLS // tn),
        in_specs=[pl.BlockSpec((tm, d), lambda i, j: (i, 0)),
                  pl.BlockSpec((HALO, d), lambda i, j: (jnp.maximum(i * per - 1, 0), 0)),
                  pl.BlockSpec((HALO, d), lambda i, j: (jnp.minimum((i + 1) * per, last), 0)),
                  pl.BlockSpec((1, 8, d), lambda i, j: (stream_of_tile(i), 0, 0)),
                  pl.BlockSpec((d, tn), lambda i, j: (0, j)),
                  pl.BlockSpec((8, 2 * ML_D), lambda i, j: (0, 0)),
                  pl.BlockSpec((1, 2 * ML_D), lambda i, j: (0, 0)),
                  pl.BlockSpec((8, SSD_XBC), lambda i, j: (0, 0)),
                  pl.BlockSpec((1, SSD_XBC), lambda i, j: (0, 0))],
        out_specs=[pl.BlockSpec((tm, tn), lambda i, j: (i, jnp.clip(j - 1, 0, nconv - 1))),
                   pl.BlockSpec((tm, tn), lambda i, j: (i, jnp.maximum(j - nconv, 0)))],
        out_shape=[jax.ShapeDtypeStruct((m, UC_COLS), F32), jax.ShapeDtypeStruct((m, U_COLS - UC_COLS), F32)],
        scratch_shapes=[pltpu.VMEM((tm + 2 * HALO, d), BF16), pltpu.VMEM((2, tm + 2 * HALO, tn), F32)],
        compiler_params=_cparams(2),
        name="in_proj",
    )(x, x, x, pack, w, cwa, cba, cwb, cbb)


def _group_norm(y, w, group):
    outs = []
    for g in range(y.shape[1] // group):
        yg = y[:, g * group:(g + 1) * group]
        ms = jnp.mean(yg * yg, axis=-1, keepdims=True)
        outs.append(yg * lax.rsqrt(ms + EPS))
    return jnp.concatenate(outs, axis=1) * w


def _merge_kernel(x_ref, pk_ref, sf_ref, sb_ref, mf_ref, mb_ref, hf_ref, hb_ref, z_ref, mo_ref, hg_ref,
                  nw_ref, wo_ref, o_ref):
    nw = nw_ref[...]
    ys = (sf_ref[...] + sb_ref[...]) * _silu(z_ref[...])
    ys = _group_norm(ys, nw[:, :SSD_D], SSD_D // SSD_GROUPS)
    ym = _group_norm(mf_ref[...] + mb_ref[...], nw[:, SSD_D:SSD_D + ML_D], ML_HEAD_DIM) * _sigmoid(mo_ref[...])
    yh = _group_norm(hf_ref[...] + hb_ref[...], nw[:, SSD_D + ML_D:], HG_DV) * _silu(hg_ref[...])
    y = jnp.concatenate([ys, ym, yh], axis=1).astype(BF16)
    o_ref[...] = x_ref[...] + pk_ref[0][2:3] * _dot(y, wo_ref[...])


def _merge(x, pack, stream_of_tile, ys, ym, yh, u, nw, wo, *, tm):
    m, d = x.shape

    def rows(width, colblk):
        return pl.BlockSpec((tm, width), lambda i: (i, colblk))

    return pl.pallas_call(
        _merge_kernel,
        grid=(m // tm,),
        in_specs=[rows(d, 0),
                  pl.BlockSpec((1, 8, d), lambda i: (stream_of_tile(i), 0, 0)),
                  rows(SSD_D, 0), rows(SSD_D, 0), rows(ML_D, 0), rows(ML_D, 0), rows(HG_D, 0), rows(HG_D, 0),
                  rows(SSD_D, (COL_Z - UC_COLS) // SSD_D), rows(ML_D, (COL_MO - UC_COLS) // ML_D),
                  rows(HG_D, (COL_HGG - UC_COLS) // HG_D),
                  pl.BlockSpec((1, d), lambda i: (0, 0)),
                  pl.BlockSpec((d, d), lambda i: (0, 0))],
        out_specs=rows(d, 0),
        out_shape=jax.ShapeDtypeStruct((m, d), F32),
        compiler_params=_cparams(1),
        name="merge_out",
    )(x, pack, ys[0], ys[1], ym[0], ym[1], yh[0], yh[1], u, u, u, nw, wo)


STEP_CHUNKS = 4


def _scan_tri(length, rev):
    r = lax.broadcasted_iota(jnp.int32, (length, length), 0)
    c = lax.broadcasted_iota(jnp.int32, (length, length), 1)
    return (c >= r) if rev else (c <= r)


def _scan_tri3(length, rev):
    hi = lax.broadcasted_iota(jnp.int32, (length // 8, 8, length), 0)
    lo = lax.broadcasted_iota(jnp.int32, (length // 8, 8, length), 1)
    c = lax.broadcasted_iota(jnp.int32, (length // 8, 8, length), 2)
    r = hi * 8 + lo
    return (c >= r) if rev else (c <= r)


def _tiles(x):
    return x.reshape(x.shape[0] // 8, 8, x.shape[1])


def _row8(x, i):
    return jnp.broadcast_to(x[i:i + 1, :], (8, x.shape[1]))[None]


def _ssd_chunk(xbc, raw, hp, dv_ref, state, y_ref, rs, rev):
    L = CHUNK
    P = SSD_HEAD_DIM
    xs = xbc[:, :SSD_D]
    bm = xbc[:, SSD_D:SSD_D + SSD_GROUPS * SSD_STATE]
    cm = xbc[:, SSD_D + SSD_GROUPS * SSD_STATE:]

    off = SM_DT + SSD_HEADS * rev
    dt = _softplus(raw + hp[0:1, :])
    la = dt * hp[1:2, :]
    a = _dot_sel(_scan_tri(L, rev).astype(BF16), la)
    mask = _scan_tri3(L, rev)
    a_end = a[0:1, :] if rev else a[L - 1:L, :]
    eend = jnp.exp(a_end)
    a_t = a.T
    dt_t = dt.T
    wf_t = (jnp.exp(a_end - a) * dt).T
    lane = lax.broadcasted_iota(jnp.int32, (1, 2 * P), 1)
    first = lane < P
    new_state = []

    for g in range(SSD_GROUPS):
        cg = cm[:, g * SSD_STATE:(g + 1) * SSD_STATE]
        bg = bm[:, g * SSD_STATE:(g + 1) * SSD_STATE]
        cb = _tiles(_dot_nt(cg.astype(BF16), bg.astype(BF16)))
        cg3 = _tiles(cg)
        bg_t = _tiles(bg.T)
        for pp in range(SSD_HEADS // SSD_GROUPS // 2):
            pair = g * (SSD_HEADS // SSD_GROUPS // 2) + pp
            xp = xs[:, pair * 2 * P:(pair + 1) * 2 * P]
            hpair = state[pair]
            lhs, rhs, blhs, xms, ends = [], [], [], [], []
            for k in range(2):
                ln = off + 2 * pair + k
                sel = first if k == 0 else jnp.logical_not(first)
                acol = _tiles(jnp.broadcast_to(a[:, ln:ln + 1], (L, L)))
                dec = jnp.exp(jnp.where(mask, acol - _row8(a_t, ln), NEG))
                lhs += [(cb * dec * _row8(dt_t, ln)).reshape(L, L), (cg3 * jnp.exp(acol)).reshape(L, SSD_STATE)]
                xm = jnp.where(sel, xp, 0.0)
                rhs += [xm, jnp.where(sel, hpair, 0.0)]
                blhs.append((bg_t * _row8(wf_t, ln)).reshape(SSD_STATE, L))
                xms.append(xm)
                ends.append(eend[:, ln:ln + 1])
            xcat = jnp.concatenate(xms, axis=0).astype(BF16)
            y = _dot(jnp.concatenate(lhs, axis=1).astype(BF16), jnp.concatenate(rhs, axis=0).astype(BF16))
            if not rev:
                y = y + dv_ref[:, pair * 2 * P:(pair + 1) * 2 * P] * xp
            y_ref[0, rs, pair * 2 * P:(pair + 1) * 2 * P] = y
            upd = _dot(jnp.concatenate(blhs, axis=1).astype(BF16), xcat)
            new_state.append(hpair * jnp.where(first, ends[0], ends[1]) + upd)
    return new_state


def _ssd_kernel(xf_ref, sf_ref, xb_ref, sb_ref, hp_ref, dv_ref, h0_ref, yf_ref, yb_ref, hout_ref, h_scr,
                *, nsteps, nsub):
    c = pl.program_id(1)

    @pl.when(c == 0)
    def _():
        h_scr[...] = h0_ref[:, 0]

    npair = SSD_HEADS // 2
    states = [[h_scr[d, p] for p in range(npair)] for d in range(2)]
    streams = ((xf_ref, sf_ref, yf_ref), (xb_ref, sb_ref, yb_ref))
    for d, (x_ref, s_ref, y_ref) in enumerate(streams):
        subs = range(nsub - 1, -1, -1) if d else range(nsub)
        for u in subs:
            rs = slice(u * CHUNK, (u + 1) * CHUNK)
            states[d] = _ssd_chunk(x_ref[0, rs, :], s_ref[0, rs, 0:128], hp_ref[d], dv_ref, states[d], y_ref, rs, d)
    for d in range(2):
        for p in range(npair):
            h_scr[d, p] = states[d][p]

    @pl.when(c == nsteps - 1)
    def _():
        hout_ref[:, 0] = h_scr[...]


def _ssd_scan(uc3, up3, hp, dv, h0):
    b, t, _ = uc3.shape
    nsub = min(STEP_CHUNKS, t // CHUNK)
    rows = nsub * CHUNK
    nsteps = t // rows
    fwd = lambda c: c
    bwd = lambda c: nsteps - 1 - c
    st_shape = h0.shape

    def stream_specs(step_of):
        return [pl.BlockSpec((1, rows, SSD_XBC), lambda i, c: (i, step_of(c), COL_XBC // SSD_XBC)),
                pl.BlockSpec((1, rows, 512), lambda i, c: (i, step_of(c), (COL_SM - UC_COLS) // 512))]

    st_spec = pl.BlockSpec((2, 1) + st_shape[2:], lambda i, c: (0, i, 0, 0, 0))
    return pl.pallas_call(
        functools.partial(_ssd_kernel, nsteps=nsteps, nsub=nsub),
        grid=(b, nsteps),
        in_specs=stream_specs(fwd) + stream_specs(bwd) + [
            pl.BlockSpec((2, 8, 128), lambda i, c: (0, 0, 0)),
            pl.BlockSpec((1, SSD_D), lambda i, c: (0, 0)),
            st_spec],
        out_specs=[pl.BlockSpec((1, rows, SSD_D), lambda i, c: (i, fwd(c), 0)),
                   pl.BlockSpec((1, rows, SSD_D), lambda i, c: (i, bwd(c), 0)),
                   st_spec],
        out_shape=[jax.ShapeDtypeStruct((b, t, SSD_D), F32), jax.ShapeDtypeStruct((b, t, SSD_D), F32),
                   jax.ShapeDtypeStruct(st_shape, F32)],
        scratch_shapes=[pltpu.VMEM((2,) + st_shape[2:], F32)],
        compiler_params=_cparams(2),
        name="ssd_scan",
    )(uc3, up3, uc3, up3, hp, dv, h0)


def _mlstm_chunk(qk, v, raw, gb, state, y_ref, rs, rev):
    L = CHUNK
    K = ML_HEAD_DIM
    logi = raw + gb[0:1, :]
    logf = _log_sigmoid(raw + gb[1:2, :])
    bcum = _dot_sel(_scan_tri(L, rev).astype(BF16), logf)
    mask = _scan_tri3(L, rev)
    b_t = bcum.T
    logi_t = logi.T
    ones = jnp.ones((L, K), BF16)
    scale = K ** -0.5
    end = 0 if rev else L - 1
    new_state = []

    for h in range(ML_HEADS):
        li = SM_IG + ML_HEADS * rev + h
        lf = SM_FG + ML_HEADS * rev + h
        qh = qk[:, h * K:(h + 1) * K]
        kh = qk[:, ML_D + h * K:ML_D + (h + 1) * K] * scale
        vaug = jnp.concatenate([v[:, h * K:(h + 1) * K].astype(BF16), ones], axis=1)
        c_old, m_old = state[h]
        brow = b_t[lf:lf + 1, :]
        irow = logi_t[li:li + 1, :]
        bcol = _tiles(jnp.broadcast_to(bcum[:, lf:lf + 1], (L, L)))
        dm = jnp.where(mask, bcol - jnp.broadcast_to(brow - irow, (8, L))[None], NEG)
        gfull = bcol + m_old
        mt = jnp.maximum(gfull, jnp.max(dm, axis=2, keepdims=True))
        qkm = _tiles(_dot_nt(qh.astype(BF16), kh.astype(BF16))) * jnp.exp(dm - mt)
        qin = _tiles(qh) * jnp.exp(gfull - mt)
        lhs = jnp.concatenate([qkm.reshape(L, L), qin.reshape(L, K)], axis=1).astype(BF16)
        rhs = jnp.concatenate([vaug, c_old.astype(BF16)], axis=0)
        s = _dot(lhs, rhs)
        den = jnp.maximum(jnp.abs(s[:, K:]), jnp.exp(-mt).reshape(L, L))
        y_ref[0, rs, h * K:(h + 1) * K] = s[:, :K] / den
        bl = brow[:, end:end + 1]
        ds = bl - brow + irow
        m_new = jnp.maximum(bl + m_old, jnp.max(ds, axis=1, keepdims=True))
        ws = jnp.exp(ds - m_new)
        keep = jnp.exp(bl + m_old - m_new)
        new_state.append((c_old * keep + _dot((kh.T * ws).astype(BF16), vaug), m_new))
    return new_state


def _mlstm_kernel(qf_ref, vf_ref, sf_ref, qb_ref, vb_ref, sb_ref, gb_ref, c0_ref, m0_ref,
                  yf_ref, yb_ref, cout_ref, mout_ref, c_scr, m_scr, *, nsteps, nsub):
    c = pl.program_id(1)

    @pl.when(c == 0)
    def _():
        c_scr[...] = c0_ref[:, 0]
        m_scr[...] = m0_ref[:, 0]

    states = [[(c_scr[d, h], m_scr[d, h][0:1, 0:1]) for h in range(ML_HEADS)] for d in range(2)]
    streams = ((qf_ref, vf_ref, sf_ref, yf_ref), (qb_ref, vb_ref, sb_ref, yb_ref))
    for d, (q_ref, v_ref, s_ref, y_ref) in enumerate(streams):
        subs = range(nsub - 1, -1, -1) if d else range(nsub)
        for u in subs:
            rs = slice(u * CHUNK, (u + 1) * CHUNK)
            states[d] = _mlstm_chunk(q_ref[0, rs, :], v_ref[0, rs, :], s_ref[0, rs, 0:128], gb_ref[d], states[d],
                                     y_ref, rs, d)
    for d in range(2):
        for h in range(ML_HEADS):
            c_scr[d, h] = states[d][h][0]
            m_scr[d, h] = jnp.broadcast_to(states[d][h][1], (8, 128))

    @pl.when(c == nsteps - 1)
    def _():
        cout_ref[:, 0] = c_scr[...]
        mout_ref[:, 0] = m_scr[...]


def _mlstm_scan(uc3, up3, gb, c0, m0):
    b, t, _ = uc3.shape
    nsub = min(STEP_CHUNKS, t // CHUNK)
    rows = nsub * CHUNK
    nsteps = t // rows
    fwd = lambda c: c
    bwd = lambda c: nsteps - 1 - c

    def stream_specs(step_of):
        return [pl.BlockSpec((1, rows, 2 * ML_D), lambda i, c: (i, step_of(c), COL_MQK // (2 * ML_D))),
                pl.BlockSpec((1, rows, ML_D), lambda i, c: (i, step_of(c), (COL_MV - UC_COLS) // ML_D)),
                pl.BlockSpec((1, rows, 512), lambda i, c: (i, step_of(c), (COL_SM - UC_COLS) // 512))]

    c_spec = pl.BlockSpec((2, 1) + c0.shape[2:], lambda i, c: (0, i, 0, 0, 0))
    m_spec = pl.BlockSpec((2, 1) + m0.shape[2:], lambda i, c: (0, i, 0, 0, 0))
    return pl.pallas_call(
        functools.partial(_mlstm_kernel, nsteps=nsteps, nsub=nsub),
        grid=(b, nsteps),
        in_specs=stream_specs(fwd) + stream_specs(bwd) + [
            pl.BlockSpec((2, 8, 128), lambda i, c: (0, 0, 0)),
            c_spec, m_spec],
        out_specs=[pl.BlockSpec((1, rows, ML_D), lambda i, c: (i, fwd(c), 0)),
                   pl.BlockSpec((1, rows, ML_D), lambda i, c: (i, bwd(c), 0)),
                   c_spec, m_spec],
        out_shape=[jax.ShapeDtypeStruct((b, t, ML_D), F32), jax.ShapeDtypeStruct((b, t, ML_D), F32),
                   jax.ShapeDtypeStruct(c0.shape, F32), jax.ShapeDtypeStruct(m0.shape, F32)],
        scratch_shapes=[pltpu.VMEM((2,) + c0.shape[2:], F32), pltpu.VMEM((2,) + m0.shape[2:], F32)],
        compiler_params=_cparams(2),
        name="mlstm_scan",
    )(uc3, up3, up3, uc3, up3, up3, gb, c0, m0)


HG_LEVELS = 7
HG_SEL_ROWS = (2 + HG_LEVELS) * HG_CHUNK


def _hgrn_consts(rev):
    L = HG_CHUNK
    pos = np.arange(L)
    p = pos[::-1] if rev else pos
    cum = (p[None, :] <= p[:, None])
    after = (p[None, :] > p[:, None])
    sels = [cum, after]
    masks = []
    for lev in range(HG_LEVELS):
        m = 1 << lev
        blk = p // (2 * m)
        late = (p % (2 * m)) >= m
        ref = blk * 2 * m + m - 1
        pj = p[None, :]
        sel = np.where(late[:, None], (pj > ref[:, None]) & (pj <= p[:, None]),
                       (pj > p[:, None]) & (pj <= ref[:, None]))
        sels.append(sel)
        masks.append((blk[:, None] == blk[None, :]) & late[:, None] & (~late)[None, :])
    masks.append(np.eye(L, dtype=bool))
    sel_all = np.concatenate(sels, axis=0).astype(np.float32)
    mask_all = np.stack(masks, axis=0).astype(np.float32)
    return jnp.asarray(sel_all, BF16), jnp.asarray(mask_all, F32)


def _hgrn_chunk(q, fr, v, lb, sel_ref, msk_ref, s_ref):
    L = HG_CHUNK
    K = HG_DK
    x1 = jnp.log(lb)
    x2 = jnp.log1p(-lb) + _log_sigmoid(fr)
    logf = jnp.maximum(x1, x2) + _log1p_exp_neg(jnp.abs(x1 - x2))
    k = (1.0 - lb) * _sigmoid(-fr)
    ex = _dot_sel(sel_ref[...], logf)
    bcum = ex[0:L]
    after = ex[L:2 * L]
    q_in = (q * jnp.exp(bcum)).astype(BF16)
    k_w = (k * jnp.exp(after)).astype(BF16)
    e_end = jnp.exp(bcum[0:1] + after[0:1])
    qb = q.astype(BF16)
    kb = k.astype(BF16)
    vb = v.astype(BF16)
    qs, ks = [], []
    for lev in range(HG_LEVELS):
        e = jnp.exp(ex[(2 + lev) * L:(3 + lev) * L])
        qs.append((q * e).astype(BF16))
        ks.append((k * e).astype(BF16))
    outs = []
    for h in range(HG_HEADS):
        sl = slice(h * K, (h + 1) * K)
        att = msk_ref[HG_LEVELS] * _dot_nt(qb[:, sl], kb[:, sl])
        for lev in range(HG_LEVELS):
            att = att + msk_ref[lev] * _dot_nt(qs[lev][:, sl], ks[lev][:, sl])
        st = s_ref[h]
        outs.append(_dot(att.astype(BF16), vb[:, sl]) + _dot_nt(q_in[:, sl], st.astype(BF16)))
        s_ref[h] = st * e_end[:, sl] + _dot_tn(vb[:, sl], k_w[:, sl])
    return jnp.concatenate(outs, axis=1)


def _hgrn_rows_kernel(q_ref, f_ref, v_ref, lb_ref, sel_ref, msk_ref, s0_ref, y_ref, sout_ref, s_scr,
                      *, nsteps, nsub, rev):
    c = pl.program_id(1)

    @pl.when(c == 0)
    def _():
        s_scr[...] = s0_ref[0]

    lb = lb_ref[...]
    for u in (range(nsub - 1, -1, -1) if rev else range(nsub)):
        rs = slice(u * HG_CHUNK, (u + 1) * HG_CHUNK)
        y_ref[0, rs, :] = _hgrn_chunk(q_ref[0, rs, :], f_ref[0, rs, :], v_ref[0, rs, :], lb, sel_ref, msk_ref, s_scr)

    @pl.when(c == nsteps - 1)
    def _():
        sout_ref[0] = s_scr[...]


def _hgrn_cols_kernel(q_ref, f_ref, v_ref, lb_ref, sel_ref, msk_ref, s0_ref, y_ref, sout_ref,
                      q_scr, f_scr, v_scr, y_scr, s_scr, *, nsteps, nsub, rev):
    c = pl.program_id(1)
    w = (nsteps - 1 - c) if rev else c
    wi = w % 8
    enter, leave = (7, 0) if rev else (0, 7)

    @pl.when(c == 0)
    def _():
        s_scr[...] = s0_ref[0]

    @pl.when(wi == enter)
    def _():
        for k in range(8):
            q_scr[k] = q_ref[0, :, 0, k, :]
            f_scr[k] = f_ref[0, :, 0, k, :]
            v_scr[k] = v_ref[0, :, 0, k, :]

    lb = lb_ref[...]
    for u in (range(nsub - 1, -1, -1) if rev else range(nsub)):
        rs = pl.ds(u * HG_CHUNK, HG_CHUNK)
        y_scr[wi, rs, :] = _hgrn_chunk(q_scr[wi, rs, :], f_scr[wi, rs, :], v_scr[wi, rs, :], lb, sel_ref, msk_ref,
                                       s_scr)

    @pl.when(wi == leave)
    def _():
        for k in range(8):
            y_ref[0, :, 0, k, :] = y_scr[k]

    @pl.when(c == nsteps - 1)
    def _():
        sout_ref[0] = s_scr[...]


def _hgrn_scan(uc, u, lb, s0, *, rev, col_major):
    sel, msk = _hgrn_consts(rev)
    s_shape = s0.shape
    const_specs = [pl.BlockSpec((1, HG_F), lambda i, c: (0, 0)),
                   pl.BlockSpec((HG_SEL_ROWS, HG_CHUNK), lambda i, c: (0, 0)),
                   pl.BlockSpec((HG_LEVELS + 1, HG_CHUNK, HG_CHUNK), lambda i, c: (0, 0, 0)),
                   pl.BlockSpec((1,) + s_shape[1:], lambda i, c: (i, 0, 0, 0))]
    s_out = pl.BlockSpec((1,) + s_shape[1:], lambda i, c: (i, 0, 0, 0))
    name = ("hgrn_cols" if col_major else "hgrn_rows") + ("_bwd" if rev else "_fwd")
    cols = ((COL_HQ - UC_COLS) // HG_F, (COL_HGF - UC_COLS) // HG_F + rev, COL_HGI // HG_F)
    if col_major:
        b, r, wg, _, _ = u.shape
        nsteps = wg * 8
        nsub = r // HG_CHUNK
        st = (lambda c: nsteps - 1 - c) if rev else (lambda c: c)
        blk = (1, r, 1, 8, HG_F)
        in_specs = [pl.BlockSpec(blk, lambda i, c, cb=cb: (i, 0, st(c) // 8, 0, cb)) for cb in cols]
        out_specs = [pl.BlockSpec(blk, lambda i, c: (i, 0, st(c) // 8, 0, 0)), s_out]
        out_shape = [jax.ShapeDtypeStruct((b, r, wg, 8, HG_D), F32), jax.ShapeDtypeStruct(s_shape, F32)]
        scratch = [pltpu.VMEM((8, r, HG_F), F32)] * 4 + [pltpu.VMEM(s_shape[1:], F32)]
        body = functools.partial(_hgrn_cols_kernel, nsteps=nsteps, nsub=nsub, rev=rev)
    else:
        b, t, _ = u.shape
        nsub = min(STEP_CHUNKS, t // HG_CHUNK)
        rows = nsub * HG_CHUNK
        nsteps = t // rows
        st = (lambda c: nsteps - 1 - c) if rev else (lambda c: c)
        in_specs = [pl.BlockSpec((1, rows, HG_F), lambda i, c, cb=cb: (i, st(c), cb)) for cb in cols]
        out_specs = [pl.BlockSpec((1, rows, HG_D), lambda i, c: (i, st(c), 0)), s_out]
        out_shape = [jax.ShapeDtypeStruct((b, t, HG_D), F32), jax.ShapeDtypeStruct(s_shape, F32)]
        scratch = [pltpu.VMEM(s_shape[1:], F32)]
        body = functools.partial(_hgrn_rows_kernel, nsteps=nsteps, nsub=nsub, rev=rev)
    return pl.pallas_call(
        body,
        grid=(b, nsteps),
        in_specs=in_specs + const_specs,
        out_specs=out_specs,
        out_shape=out_shape,
        scratch_shapes=scratch,
        compiler_params=_cparams(2),
        name=name,
    )(u, u, uc, lb, sel, msk, s0)


def _pad_rows(a, rows):
    return jnp.concatenate([a, jnp.zeros((rows - a.shape[0],) + a.shape[1:], a.dtype)], axis=0)


def _lane_vec(parts, width=128):
    out = jnp.zeros((width,), F32)
    for off, vals in parts:
        out = lax.dynamic_update_slice(out, vals.astype(F32), (off,))
    return out


def _permute_w_in(w_in):
    offs = np.cumsum([0, SSD_D, SSD_XBC, 2 * SSD_HEADS, 2 * ML_D, ML_D, ML_D, 2 * ML_HEADS, 2 * ML_HEADS,
                      HG_F, 2 * HG_F, HG_D, HG_D])
    z, xbc, dt, mqk, mv, mo, ig, fg, hq, hf, hi, hg = [w_in[:, offs[i]:offs[i + 1]] for i in range(12)]
    pad = jnp.zeros((w_in.shape[0], 512 - (dt.shape[1] + ig.shape[1] + fg.shape[1])), w_in.dtype)
    parts = [mqk, hi, xbc, z, hf, mv, mo, hq, hg, dt, ig, fg, pad]
    return jnp.concatenate([p.astype(BF16) for p in parts], axis=1)


def kernel(x, c, ctx, c_ctx, ada_w, ada_b, norm_w, ffn_w_gate, ffn_w_up, ffn_w_down, w_in, w_out,
           ssd_conv_w, ssd_conv_b, ssd_a_log, ssd_dt_bias, ssd_d, ssd_norm_w, ml_conv_w, ml_conv_b,
           ml_ig_b, ml_fg_b, ml_norm_w, hg_lb_logits, hg_norm_w, final_norm_w):
    bsz, seq, d = x.shape
    ctx_len = ctx.shape[1]
    depth = ada_w.shape[0]
    rows = seq // GRID_W
    n_stream = bsz + 1

    pr = jax.nn.softmax(hg_lb_logits.astype(F32), axis=0)
    cum = jnp.cumsum(pr, axis=0)
    lb_all = cum - cum[0]

    cvec = _pad_rows(jnp.concatenate([c, c_ctx[None, :]], axis=0), 8)
    mods = _ada_mods(cvec, ada_w, ada_b)

    xl = x.reshape(bsz * seq, d)
    xc = ctx.reshape(bsz * ctx_len, d)
    tm_l = 512
    tm_c = min(512, bsz * ctx_len)
    lat_stream = lambda tm: (lambda i: i // (seq // tm))
    ctx_stream = lambda tm: (lambda i: bsz)
    fw = final_norm_w.reshape(1, d).astype(F32)
    zeros = lambda shape: jnp.zeros(shape, F32)

    for l in range(depth):
        need_ctx = l < depth - 1
        mod = mods[l, :n_stream].reshape(n_stream, 3, 3, d)

        def pack(sub):
            nw = jnp.broadcast_to(norm_w[l, sub][None, None, :], (n_stream, 1, d))
            return jnp.concatenate([mod[:, sub], nw, zeros((n_stream, 4, d))], axis=1)

        wg = [ffn_w_gate[l, s].astype(BF16) for s in (0, 1)]
        wu = [ffn_w_up[l, s].astype(BF16) for s in (0, 1)]
        wd = [ffn_w_down[l, s].astype(BF16) for s in (0, 1)]

        pk0 = pack(0)
        xl = _ffn(xl, pk0, lat_stream(tm_l), wg[0], wu[0], wd[0], fw, tm=tm_l, tf=512)
        xc = _ffn(xc, pk0, ctx_stream(tm_c), wg[0], wu[0], wd[0], fw, tm=tm_c, tf=512)

        pk1 = pack(1)
        wp = _permute_w_in(w_in[l])
        s_cw = _pad_rows(ssd_conv_w[l].astype(F32), 8)
        s_cb = ssd_conv_b[l].reshape(1, SSD_XBC).astype(F32)
        m_cw = _pad_rows(ml_conv_w[l].astype(F32), 8)
        m_cb = ml_conv_b[l].reshape(1, 2 * ML_D).astype(F32)
        ucl, ul = _proj(xl, pk1, lat_stream(tm_l), wp, m_cw, m_cb, s_cw, s_cb, tm=tm_l, tn=1536, seq=seq)
        ucc, uc = _proj(xc, pk1, ctx_stream(ctx_len), wp, m_cw, m_cb, s_cw, s_cb, tm=ctx_len, tn=1536, seq=ctx_len)
        ucl, ul = (a.reshape(bsz, seq, a.shape[-1]) for a in (ucl, ul))
        ucc, uc = (a.reshape(bsz, ctx_len, a.shape[-1]) for a in (ucc, uc))

        a_neg = -jnp.exp(ssd_a_log[l].astype(F32))
        dv = jnp.repeat(ssd_d[l].astype(F32), SSD_HEAD_DIM).reshape(1, SSD_D)
        hp = jnp.stack([_pad_rows(jnp.stack([_lane_vec([(SM_DT + SSD_HEADS * r, ssd_dt_bias[l, r])]),
                                             _lane_vec([(SM_DT + SSD_HEADS * r, a_neg[r])])]), 8)
                        for r in (0, 1)])
        h0 = zeros((2, bsz, SSD_HEADS // 2, SSD_STATE, 2 * SSD_HEAD_DIM))
        ysc_f, ysc_b, hc = _ssd_scan(ucc, uc, hp, dv, h0)
        ysl_f, ysl_b, _ = _ssd_scan(ucl, ul, hp, dv, hc)

        gb =jnp.stack([_pad_rows(jnp.stack([_lane_vec([(SM_IG + ML_HEADS * r, ml_ig_b[l, r])]),
                                             _lane_vec([(SM_FG + ML_HEADS * r, ml_fg_b[l, r])])]), 8)
                        for r in (0, 1)])
        c0 = zeros((2, bsz, ML_HEADS, ML_HEAD_DIM, 2 * ML_HEAD_DIM))
        m0 = zeros((2, bsz, ML_HEADS, 8, 128))
        ymc_f, ymc_b, cc, mc = _mlstm_scan(ucc, uc, gb, c0, m0)
        yml_f, yml_b, _, _ = _mlstm_scan(ucl, ul, gb, cc, mc)

        lb = lb_all[l].reshape(1, HG_F)
        grid5 = lambda a: a.reshape(bsz, rows, GRID_W // 8, 8, a.shape[-1])
        yh_l, yh_c = [], []
        for rev in (0, 1):
            s0 = zeros((bsz, HG_HEADS, HG_DV, HG_DK))
            yc_, sc = _hgrn_scan(ucc, uc, lb, s0, rev=rev, col_major=False)
            yl_, _ = _hgrn_scan(grid5(ucl), grid5(ul), lb, sc, rev=rev, col_major=True)
            yh_c.append(yc_.reshape(bsz * ctx_len, HG_D))
            yh_l.append(yl_.reshape(bsz * seq, HG_D))

        nw = jnp.concatenate([ssd_norm_w[l], ml_norm_w[l], hg_norm_w[l]]).reshape(1, d).astype(F32)
        wo = w_out[l].astype(BF16)
        flat = lambda a: a.reshape(-1, a.shape[-1])
        xl = _merge(xl, pk1, lat_stream(256), [flat(ysl_f), flat(ysl_b)], [flat(yml_f), flat(yml_b)], yh_l,
                    flat(ul), nw, wo, tm=256)

        pk2 = pack(2)
        xl = _ffn(xl, pk2, lat_stream(tm_l), wg[1], wu[1], wd[1], fw, tm=tm_l, tf=512, final=not need_ctx)
        if need_ctx:
            xc = _merge(xc, pk1, ctx_stream(256), [flat(ysc_f), flat(ysc_b)], [flat(ymc_f), flat(ymc_b)], yh_c,
                        flat(uc), nw, wo, tm=256)
            xc = _ffn(xc, pk2, ctx_stream(tm_c), wg[1], wu[1], wd[1], fw, tm=tm_c, tf=512)

    return xl.reshape(bsz, seq, d)
```

```python
import functools

import numpy as np
import jax
import jax.numpy as jnp
from jax import lax
from jax.experimental import pallas as pl
from jax.experimental.pallas import tpu as pltpu

F32 = jnp.float32
BF16 = jnp.bfloat16

GRID_W = 64
FFN_HALF = 0.5
EPS = 1e-6
CHUNK = 128
HG_CHUNK = 128
CONV_W = 5

SSD_HEADS = 16
SSD_HEAD_DIM = 64
SSD_GROUPS = 2
SSD_STATE = 128
ML_HEADS = 4
ML_HEAD_DIM = 128
HG_HEADS = 4
HG_DK = 128
HG_DV = 128

SSD_D = SSD_HEADS * SSD_HEAD_DIM
SSD_XBC = SSD_D + 2 * SSD_GROUPS * SSD_STATE
ML_D = ML_HEADS * ML_HEAD_DIM
HG_F = HG_HEADS * HG_DK
HG_D = HG_HEADS * HG_DV

COL_MQK = 0
COL_HGI = 1024
COL_XBC = 1536
COL_Z = 3072
COL_HGF = 4096
COL_MV = 5120
COL_MO = 5632
COL_HQ = 6144
COL_HGG = 6656
COL_SM = 7168
U_COLS = 7680
UC_COLS = 3072
SM_DT = 0
SM_IG = 2 * SSD_HEADS
SM_FG = 2 * SSD_HEADS + 2 * ML_HEADS

NEG = -1e30
VMEM_LIMIT_BYTES = 56 * 1024 * 1024


def _cparams(n_axes):
    return pltpu.CompilerParams(dimension_semantics=("arbitrary",) * n_axes,
                                vmem_limit_bytes=VMEM_LIMIT_BYTES)


def _sigmoid(x):
    return 1.0 / (1.0 + jnp.exp(-x))


def _silu(x):
    return x * _sigmoid(x)


def _log1p_exp_neg(y):
    return jnp.log(1.0 + jnp.exp(-y))


def _log_sigmoid(x):
    return jnp.minimum(x, 0.0) - _log1p_exp_neg(jnp.abs(x))


def _softplus(x):
    return jnp.maximum(x, 0.0) + jnp.log1p(jnp.exp(-jnp.abs(x)))


def _dot(a, b):
    return jnp.dot(a, b, preferred_element_type=F32)


def _dot_nt(a, b):
    return lax.dot_general(a, b, (((1,), (1,)), ((), ())), preferred_element_type=F32)


def _dot_tn(a, b):
    return lax.dot_general(a, b, (((0,), (0,)), ((), ())), preferred_element_type=F32)


def _dot_sel(sel_bf16, x):
    hi = x.astype(BF16)
    r1 = x - hi.astype(F32)
    mid = r1.astype(BF16)
    lo = (r1 - mid.astype(F32)).astype(BF16)
    return _dot(sel_bf16, hi) + _dot(sel_bf16, mid) + _dot(sel_bf16, lo)


def _norm_mod(x, pk):
    ms = jnp.mean(x * x, axis=-1, keepdims=True)
    xn = (x * lax.rsqrt(ms + EPS)) * pk[3:4]
    return xn * (1.0 + pk[1:2]) + pk[0:1]


def _ada_kernel(c_ref, w_ref, b_ref, o_ref):
    c = _silu(c_ref[...]).astype(BF16)
    o_ref[0] = _dot(c, w_ref[0].astype(BF16)) + b_ref[0]


def _ada_mods(cvec, ada_w, ada_b):
    depth, d, n = ada_w.shape
    tn = 1024
    return pl.pallas_call(
        _ada_kernel,
        grid=(depth, n // tn),
        in_specs=[pl.BlockSpec((8, d), lambda l, j: (0, 0)),
                  pl.BlockSpec((1, d, tn), lambda l, j: (l, 0, j)),
                  pl.BlockSpec((1, 1, tn), lambda l, j: (l, 0, j))],
        out_specs=pl.BlockSpec((1, 8, tn), lambda l, j: (l, 0, j)),
        out_shape=jax.ShapeDtypeStruct((depth, 8, n), F32),
        compiler_params=_cparams(2),
        name="ada_mods",
    )(cvec, ada_w, ada_b.reshape(depth, 1, n))


def _ffn_kernel(x_ref, pk_ref, wg_ref, wu_ref, wd_ref, fw_ref, o_ref, h_scr, acc_scr, *, nf, final):
    f = pl.program_id(1)

    @pl.when(f == 0)
    def _():
        h_scr[...] = _norm_mod(x_ref[...], pk_ref[0]).astype(BF16)
        acc_scr[...] = jnp.zeros_like(acc_scr)

    h = h_scr[...]
    g = _dot(h, wg_ref[...])
    u = _dot(h, wu_ref[...])
    a = (_silu(g) * u).astype(BF16)
    acc_scr[...] += _dot(a, wd_ref[...])

    @pl.when(f == nf - 1)
    def _():
        y = x_ref[...] + (FFN_HALF * pk_ref[0][2:3]) * acc_scr[...]
        if final:
            ms = jnp.mean(y * y, axis=-1, keepdims=True)
            y = (y * lax.rsqrt(ms + EPS)) * fw_ref[...]
        o_ref[...] = y


def _ffn(x, pack, stream_of_tile, wg, wu, wd, ls, fw, *, tm, tf, final=False):
    m, d = x.shape
    l, s = ls
    nf = wg.shape[-1] // tf
    return pl.pallas_call(
        functools.partial(_ffn_kernel, nf=nf, final=final),
        grid=(m // tm, nf),
        in_specs=[pl.BlockSpec((tm, d), lambda i, f: (i, 0)),
                  pl.BlockSpec((1, 8, d), lambda i, f: (stream_of_tile(i), 0, 0)),
                  pl.BlockSpec((None, None, d, tf), lambda i, f: (l, s, 0, f)),
                  pl.BlockSpec((None, None, d, tf), lambda i, f: (l, s, 0, f)),
                  pl.BlockSpec((None, None, tf, d), lambda i, f: (l, s, f, 0)),
                  pl.BlockSpec((1, d), lambda i, f: (0, 0))],
        out_specs=pl.BlockSpec((tm, d), lambda i, f: (i, 0)),
        out_shape=jax.ShapeDtypeStruct((m, d), F32),
        scratch_shapes=[pltpu.VMEM((tm, d), BF16), pltpu.VMEM((tm, d), F32)],
        compiler_params=_cparams(2),
        name="ffn",
    )(x, pack, wg, wu, wd, fw)


HALO = 16


def _proj_kernel(x_ref, xp_ref, xn_ref, pk_ref, w_ref, cwa_ref, cba_ref, cwb_ref, cbb_ref, oc_ref, o_ref,
                 h_scr, raw_scr, *, tm, tiles_per_seq):
    i = pl.program_id(0)
    j = pl.program_id(1)
    tn = o_ref.shape[1]

    def conv_tile(slot, cw_ref, cb_ref, ncols):
        keep_p = (i % tiles_per_seq != 0).astype(F32)
        keep_n = (i % tiles_per_seq != tiles_per_seq - 1).astype(F32)
        rb, cb = 64, 256
        for c0 in range(0, ncols, cb):
            w8 = [jnp.broadcast_to(cw_ref[t:t + 1, c0:c0 + cb], (8, cb))[None] for t in range(CONV_W)]
            b8 = jnp.broadcast_to(cb_ref[:, c0:c0 + cb], (8, cb))[None]
            for r0 in range(0, tm, rb):
                blk = raw_scr[slot, HALO + r0 - 8:HALO + r0 + rb + 8, c0:c0 + cb]
                if r0 == 0:
                    blk = jnp.concatenate([blk[:8] * keep_p, blk[8:]], axis=0)
                if r0 == tm - rb:
                    blk = jnp.concatenate([blk[:rb + 8], blk[rb + 8:] * keep_n], axis=0)
                acc = b8
                for t in range(CONV_W):
                    sh = blk if t == CONV_W // 2 else pltpu.roll(blk, (CONV_W // 2 - t) % (rb + 16), axis=0)
                    acc = acc + _tiles(sh[8:8 + rb]) * w8[t]
                oc_ref[r0:r0 + rb, c0:c0 + cb] = _silu(acc).reshape(rb, cb)
        if ncols < tn:
            oc_ref[:, ncols:] = raw_scr[slot, HALO:HALO + tm, ncols:]

    @pl.when(j == 0)
    def _():
        pk = pk_ref[0]
        h_scr[0:HALO, :] = _norm_mod(xp_ref[...], pk).astype(BF16)
        h_scr[HALO:HALO + tm, :] = _norm_mod(x_ref[...], pk).astype(BF16)
        h_scr[HALO + tm:, :] = _norm_mod(xn_ref[...], pk).astype(BF16)
        raw_scr[0] = _dot(h_scr[...], w_ref[...])

    @pl.when(j == 1)
    def _():
        conv_tile(0, cwa_ref, cba_ref, 2 * ML_D)
        raw_scr[1] = _dot(h_scr[...], w_ref[...])

    @pl.when(j == 2)
    def _():
        conv_tile(1, cwb_ref, cbb_ref, SSD_XBC)
        o_ref[...] = _dot(h_scr[HALO:HALO + tm, :], w_ref[...])

    @pl.when(j > 2)
    def _():
        o_ref[...] = _dot(h_scr[HALO:HALO + tm, :], w_ref[...])


def _proj(x, pack, stream_of_tile, w, layer, cwa, cba, cwb, cbb, *, tm, tn, seq):
    m, d = x.shape
    assert (COL_MQK, COL_XBC, UC_COLS) == (0, tn, 2 * tn)
    per = tm // HALO
    last = m // HALO - 1
    nconv = UC_COLS // tn
    return pl.pallas_call(
        functools.partial(_proj_kernel, tm=tm, tiles_per_seq=seq // tm),
        grid=(m // tm, U_COLS // tn),
        in_specs=[pl.BlockSpec((tm, d), lambda i, j: (i, 0)),
                  pl.BlockSpec((HALO, d), lambda i, j: (jnp.maximum(i * per - 1, 0), 0)),
                  pl.BlockSpec((HALO, d), lambda i, j: (jnp.minimum((i + 1) * per, last), 0)),
                  pl.BlockSpec((1, 8, d), lambda i, j: (stream_of_tile(i), 0, 0)),
                  pl.BlockSpec((None, d, tn), lambda i, j: (layer, 0, j)),
                  pl.BlockSpec((8, 2 * ML_D), lambda i, j: (0, 0)),
                  pl.BlockSpec((1, 2 * ML_D), lambda i, j: (0, 0)),
                  pl.BlockSpec((8, SSD_XBC), lambda i, j: (0, 0)),
                  pl.BlockSpec((1, SSD_XBC), lambda i, j: (0, 0))],
        out_specs=[pl.BlockSpec((tm, tn), lambda i, j: (i, jnp.clip(j - 1, 0, nconv - 1))),
                   pl.BlockSpec((tm, tn), lambda i, j: (i, jnp.maximum(j - nconv, 0)))],
        out_shape=[jax.ShapeDtypeStruct((m, UC_COLS), F32), jax.ShapeDtypeStruct((m, U_COLS - UC_COLS), F32)],
        scratch_shapes=[pltpu.VMEM((tm + 2 * HALO, d), BF16), pltpu.VMEM((2, tm + 2 * HALO, tn), F32)],
        compiler_params=_cparams(2),
        name="in_proj",
    )(x, x, x, pack, w, cwa, cba, cwb, cbb)


def _group_norm(y, w, group):
    outs = []
    for g in range(y.shape[1] // group):
        yg = y[:, g * group:(g + 1) * group]
        ms = jnp.mean(yg * yg, axis=-1, keepdims=True)
        outs.append(yg * lax.rsqrt(ms + EPS))
    return jnp.concatenate(outs, axis=1) * w


def _merge_kernel(x_ref, pk_ref, sf_ref, sb_ref, mf_ref, mb_ref, hf_ref, hb_ref, z_ref, mo_ref, hg_ref,
                  nw_ref, wo_ref, o_ref):
    nw = nw_ref[...]
    ys = (sf_ref[...] + sb_ref[...]) * _silu(z_ref[...])
    ys = _group_norm(ys, nw[:, :SSD_D], SSD_D // SSD_GROUPS)
    ym = _group_norm(mf_ref[...] + mb_ref[...], nw[:, SSD_D:SSD_D + ML_D], ML_HEAD_DIM) * _sigmoid(mo_ref[...])
    yh = _group_norm(hf_ref[...] + hb_ref[...], nw[:, SSD_D + ML_D:], HG_DV) * _silu(hg_ref[...])
    y = jnp.concatenate([ys, ym, yh], axis=1).astype(BF16)
    o_ref[...] = x_ref[...] + pk_ref[0][2:3] * _dot(y, wo_ref[...])


def _merge(x, pack, stream_of_tile, ys, ym, yh, u, nw, wo, layer, *, tm):
    m, d = x.shape

    def rows(width, colblk):
        return pl.BlockSpec((tm, width), lambda i: (i, colblk))

    return pl.pallas_call(
        _merge_kernel,
        grid=(m // tm,),
        in_specs=[rows(d, 0),
                  pl.BlockSpec((1, 8, d), lambda i: (stream_of_tile(i), 0, 0)),
                  rows(SSD_D, 0), rows(SSD_D, 0), rows(ML_D, 0), rows(ML_D, 0), rows(HG_D, 0), rows(HG_D, 0),
                  rows(SSD_D, (COL_Z - UC_COLS) // SSD_D), rows(ML_D, (COL_MO - UC_COLS) // ML_D),
                  rows(HG_D, (COL_HGG - UC_COLS) // HG_D),
                  pl.BlockSpec((1, d), lambda i: (0, 0)),
                  pl.BlockSpec((None, d, d), lambda i: (layer, 0, 0))],
        out_specs=rows(d, 0),
        out_shape=jax.ShapeDtypeStruct((m, d), F32),
        compiler_params=_cparams(1),
        name="merge_out",
    )(x, pack, ys[0], ys[1], ym[0], ym[1], yh[0], yh[1], u, u, u, nw, wo)


STEP_CHUNKS = 4


def _scan_tri(length, rev):
    r = lax.broadcasted_iota(jnp.int32, (length, length), 0)
    c = lax.broadcasted_iota(jnp.int32, (length, length), 1)
    return (c >= r) if rev else (c <= r)


def _scan_tri3(length, rev):
    hi = lax.broadcasted_iota(jnp.int32, (length // 8, 8, length), 0)
    lo = lax.broadcasted_iota(jnp.int32, (length // 8, 8, length), 1)
    c = lax.broadcasted_iota(jnp.int32, (length // 8, 8, length), 2)
    r = hi * 8 + lo
    return (c >= r) if rev else (c <= r)


def _tiles(x):
    return x.reshape(x.shape[0] // 8, 8, x.shape[1])


def _row8(x, i):
    return jnp.broadcast_to(x[i:i + 1, :], (8, x.shape[1]))[None]


def _ssd_chunk(xbc, raw, hp, dv_ref, state, y_ref, rs, rev):
    L = CHUNK
    P = SSD_HEAD_DIM
    xs = xbc[:, :SSD_D]
    bm = xbc[:, SSD_D:SSD_D + SSD_GROUPS * SSD_STATE]
    cm = xbc[:, SSD_D + SSD_GROUPS * SSD_STATE:]

    off = SM_DT + SSD_HEADS * rev
    dt = _softplus(raw + hp[0:1, :])
    la = dt * hp[1:2, :]
    a = _dot_sel(_scan_tri(L, rev).astype(BF16), la)
    mask = _scan_tri3(L, rev)
    a_end = a[0:1, :] if rev else a[L - 1:L, :]
    eend = jnp.exp(a_end)
    a_t = a.T
    dt_t = dt.T
    wf_t = (jnp.exp(a_end - a) * dt).T
    lane = lax.broadcasted_iota(jnp.int32, (1, 2 * P), 1)
    first = lane < P
    new_state = []

    for g in range(SSD_GROUPS):
        cg = cm[:, g * SSD_STATE:(g + 1) * SSD_STATE]
        bg = bm[:, g * SSD_STATE:(g + 1) * SSD_STATE]
        cb = _tiles(_dot_nt(cg.astype(BF16), bg.astype(BF16)))
        cg3 = _tiles(cg)
        bg_t = _tiles(bg.T)
        for pp in range(SSD_HEADS // SSD_GROUPS // 2):
            pair = g * (SSD_HEADS // SSD_GROUPS // 2) + pp
            xp = xs[:, pair * 2 * P:(pair + 1) * 2 * P]
            hpair = state[pair]
            lhs, rhs, blhs, xms, ends = [], [], [], [], []
            for k in range(2):
                ln = off + 2 * pair + k
                sel = first if k == 0 else jnp.logical_not(first)
                acol = _tiles(jnp.broadcast_to(a[:, ln:ln + 1], (L, L)))
                dec = jnp.exp(jnp.where(mask, acol - _row8(a_t, ln), NEG))
                lhs += [(cb * dec * _row8(dt_t, ln)).reshape(L, L), (cg3 * jnp.exp(acol)).reshape(L, SSD_STATE)]
                xm = jnp.where(sel, xp, 0.0)
                rhs += [xm, jnp.where(sel, hpair, 0.0)]
                blhs.append((bg_t * _row8(wf_t, ln)).reshape(SSD_STATE, L))
                xms.append(xm)
                ends.append(eend[:, ln:ln + 1])
            xcat = jnp.concatenate(xms, axis=0).astype(BF16)
            y = _dot(jnp.concatenate(lhs, axis=1).astype(BF16), jnp.concatenate(rhs, axis=0).astype(BF16))
            if not rev:
                y = y + dv_ref[:, pair * 2 * P:(pair + 1) * 2 * P] * xp
            y_ref[0, rs, pair * 2 * P:(pair + 1) * 2 * P] = y
            upd = _dot(jnp.concatenate(blhs, axis=1).astype(BF16), xcat)
            new_state.append(hpair * jnp.where(first, ends[0], ends[1]) + upd)
    return new_state


def _ssd_kernel(xf_ref, sf_ref, xb_ref, sb_ref, hp_ref, dv_ref, h0_ref, yf_ref, yb_ref, hout_ref, h_scr,
                *, nsteps, nsub):
    c = pl.program_id(1)

    @pl.when(c == 0)
    def _():
        h_scr[...] = h0_ref[:, 0]

    npair = SSD_HEADS // 2
    states = [[h_scr[d, p] for p in range(npair)] for d in range(2)]
    streams = ((xf_ref, sf_ref, yf_ref), (xb_ref, sb_ref, yb_ref))
    for d, (x_ref, s_ref, y_ref) in enumerate(streams):
        subs = range(nsub - 1, -1, -1) if d else range(nsub)
        for u in subs:
            rs = slice(u * CHUNK, (u + 1) * CHUNK)
            states[d] = _ssd_chunk(x_ref[0, rs, :], s_ref[0, rs, 0:128], hp_ref[d], dv_ref, states[d], y_ref, rs, d)
    for d in range(2):
        for p in range(npair):
            h_scr[d, p] = states[d][p]

    @pl.when(c == nsteps - 1)
    def _():
        hout_ref[:, 0] = h_scr[...]


def _ssd_scan(uc3, up3, hp, dv, h0):
    b, t, _ = uc3.shape
    nsub = min(STEP_CHUNKS, t // CHUNK)
    rows = nsub * CHUNK
    nsteps = t // rows
    fwd = lambda c: c
    bwd = lambda c: nsteps - 1 - c
    st_shape = h0.shape

    def stream_specs(step_of):
        return [pl.BlockSpec((1, rows, SSD_XBC), lambda i, c: (i, step_of(c), COL_XBC // SSD_XBC)),
                pl.BlockSpec((1, rows, 512), lambda i, c: (i, step_of(c), (COL_SM - UC_COLS) // 512))]

    st_spec = pl.BlockSpec((2, 1) + st_shape[2:], lambda i, c: (0, i, 0, 0, 0))
    return pl.pallas_call(
        functools.partial(_ssd_kernel, nsteps=nsteps, nsub=nsub),
        grid=(b, nsteps),
        in_specs=stream_specs(fwd) + stream_specs(bwd) + [
            pl.BlockSpec((2, 8, 128), lambda i, c: (0, 0, 0)),
            pl.BlockSpec((1, SSD_D), lambda i, c: (0, 0)),
            st_spec],
        out_specs=[pl.BlockSpec((1, rows, SSD_D), lambda i, c: (i, fwd(c), 0)),
                   pl.BlockSpec((1, rows, SSD_D), lambda i, c: (i, bwd(c), 0)),
                   st_spec],
        out_shape=[jax.ShapeDtypeStruct((b, t, SSD_D), F32), jax.ShapeDtypeStruct((b, t, SSD_D), F32),
                   jax.ShapeDtypeStruct(st_shape, F32)],
        scratch_shapes=[pltpu.VMEM((2,) + st_shape[2:], F32)],
        compiler_params=_cparams(2),
        name="ssd_scan",
    )(uc3, up3, uc3, up3, hp, dv, h0)


def _mlstm_prelude(raw, gb, rev):
    logi = raw + gb[0:1, :]
    logf = _log_sigmoid(raw + gb[1:2, :])
    bcum = _dot_sel(_scan_tri(CHUNK, rev).astype(BF16), logf)
    return bcum, bcum.T, logi.T


def _mlstm_head(qh, kh, vh, pre, h, state, rev):
    L = CHUNK
    K = ML_HEAD_DIM
    bcum, b_t, logi_t = pre
    li = SM_IG + ML_HEADS * rev + h
    lf = SM_FG + ML_HEADS * rev + h
    end = 0 if rev else L - 1
    mask = _scan_tri3(L, rev)
    kh = kh * K ** -0.5
    vaug = jnp.concatenate([vh.astype(BF16), jnp.ones((L, K), BF16)], axis=1)
    c_old, m_old = state
    brow = b_t[lf:lf + 1, :]
    irow = logi_t[li:li + 1, :]
    bcol = _tiles(jnp.broadcast_to(bcum[:, lf:lf + 1], (L, L)))
    dm = jnp.where(mask, bcol - jnp.broadcast_to(brow - irow, (8, L))[None], NEG)
    gfull = bcol + m_old
    mt = jnp.maximum(gfull, jnp.max(dm, axis=2, keepdims=True))
    qkm = _tiles(_dot_nt(qh.astype(BF16), kh.astype(BF16))) * jnp.exp(dm - mt)
    qin = _tiles(qh) * jnp.exp(gfull - mt)
    lhs = jnp.concatenate([qkm.reshape(L, L), qin.reshape(L, K)], axis=1).astype(BF16)
    rhs = jnp.concatenate([vaug, c_old.astype(BF16)], axis=0)
    s = _dot(lhs, rhs)
    den = jnp.maximum(jnp.abs(s[:, K:]), jnp.exp(-mt).reshape(L, L))
    bl = brow[:, end:end + 1]
    ds = bl - brow + irow
    m_new = jnp.maximum(bl + m_old, jnp.max(ds, axis=1, keepdims=True))
    ws = jnp.exp(ds - m_new)
    keep = jnp.exp(bl + m_old - m_new)
    c_new = c_old * keep + _dot((kh.T * ws).astype(BF16), vaug)
    return s[:, :K] / den, (c_new, m_new)


def _mlstm_kernel(qf_ref, vf_ref, sf_ref, qb_ref, vb_ref, sb_ref, gb_ref, c0_ref, m0_ref,
                  yf_ref, yb_ref, cout_ref, mout_ref, c_scr, m_scr, *, nsteps, nsub):
    c = pl.program_id(1)
    K = ML_HEAD_DIM

    @pl.when(c == 0)
    def _():
        c_scr[...] = c0_ref[:, 0]
        m_scr[...] = m0_ref[:, 0]

    states = [[(c_scr[d, h], m_scr[d, h][0:1, 0:1]) for h in range(ML_HEADS)] for d in range(2)]
    streams = ((qf_ref, vf_ref, sf_ref, yf_ref), (qb_ref, vb_ref, sb_ref, yb_ref))
    for k in range(nsub):
        ctx = []
        for d, (q_ref, v_ref, s_ref, y_ref) in enumerate(streams):
            u = nsub - 1 - k if d else k
            rs = slice(u * CHUNK, (u + 1) * CHUNK)
            ctx.append((rs, _mlstm_prelude(s_ref[0, rs, 0:128], gb_ref[d], d)))
        for h in range(ML_HEADS):
            for d, (q_ref, v_ref, s_ref, y_ref) in enumerate(streams):
                rs, pre = ctx[d]
                y, states[d][h] = _mlstm_head(q_ref[0, rs, h * K:(h + 1) * K],
                                              q_ref[0, rs, ML_D + h * K:ML_D + (h + 1) * K],
                                              v_ref[0, rs, h * K:(h + 1) * K], pre, h, states[d][h], d)
                y_ref[0, rs, h * K:(h + 1) * K] = y
    for d in range(2):
        for h in range(ML_HEADS):
            c_scr[d, h] = states[d][h][0]
            m_scr[d, h] = jnp.broadcast_to(states[d][h][1], (8, 128))

    @pl.when(c == nsteps - 1)
    def _():
        cout_ref[:, 0] = c_scr[...]
        mout_ref[:, 0] = m_scr[...]


def _mlstm_scan(uc3, up3, gb, c0, m0):
    b, t, _ = uc3.shape
    nsub = min(STEP_CHUNKS, t // CHUNK)
    rows = nsub * CHUNK
    nsteps = t // rows
    fwd = lambda c: c
    bwd = lambda c: nsteps - 1 - c

    def stream_specs(step_of):
        return [pl.BlockSpec((1, rows, 2 * ML_D), lambda i, c: (i, step_of(c), COL_MQK // (2 * ML_D))),
                pl.BlockSpec((1, rows, ML_D), lambda i, c: (i, step_of(c), (COL_MV - UC_COLS) // ML_D)),
                pl.BlockSpec((1, rows, 512), lambda i, c: (i, step_of(c), (COL_SM - UC_COLS) // 512))]

    c_spec = pl.BlockSpec((2, 1) + c0.shape[2:], lambda i, c: (0, i, 0, 0, 0))
    m_spec = pl.BlockSpec((2, 1) + m0.shape[2:], lambda i, c: (0, i, 0, 0, 0))
    return pl.pallas_call(
        functools.partial(_mlstm_kernel, nsteps=nsteps, nsub=nsub),
        grid=(b, nsteps),
        in_specs=stream_specs(fwd) + stream_specs(bwd) + [
            pl.BlockSpec((2, 8, 128), lambda i, c: (0, 0, 0)),
            c_spec, m_spec],
        out_specs=[pl.BlockSpec((1, rows, ML_D), lambda i, c: (i, fwd(c), 0)),
                   pl.BlockSpec((1, rows, ML_D), lambda i, c: (i, bwd(c), 0)),
                   c_spec, m_spec],
        out_shape=[jax.ShapeDtypeStruct((b, t, ML_D), F32), jax.ShapeDtypeStruct((b, t, ML_D), F32),
                   jax.ShapeDtypeStruct(c0.shape, F32), jax.ShapeDtypeStruct(m0.shape, F32)],
        scratch_shapes=[pltpu.VMEM((2,) + c0.shape[2:], F32), pltpu.VMEM((2,) + m0.shape[2:], F32)],
        compiler_params=_cparams(2),
        name="mlstm_scan",
    )(uc3, up3, up3, uc3, up3, up3, gb, c0, m0)


HG_LEVELS = 7
HG_SEL_ROWS = (2 + HG_LEVELS) * HG_CHUNK


def _hgrn_consts(rev):
    L = HG_CHUNK
    pos = np.arange(L)
    p = pos[::-1] if rev else pos
    cum = (p[None, :] <= p[:, None])
    after = (p[None, :] > p[:, None])
    sels = [cum, after]
    masks = []
    for lev in range(HG_LEVELS):
        m = 1 << lev
        blk = p // (2 * m)
        late = (p % (2 * m)) >= m
        ref = blk * 2 * m + m - 1
        pj = p[None, :]
        sel = np.where(late[:, None], (pj > ref[:, None]) & (pj <= p[:, None]),
                       (pj > p[:, None]) & (pj <= ref[:, None]))
        sels.append(sel)
        masks.append((blk[:, None] == blk[None, :]) & late[:, None] & (~late)[None, :])
    masks.append(np.eye(L, dtype=bool))
    sel_all = np.concatenate(sels, axis=0).astype(np.float32)
    mask_all = np.stack(masks, axis=0).astype(np.float32)
    return jnp.asarray(sel_all, BF16), jnp.asarray(mask_all, F32)


def _hgrn_chunk(q, fr, v, lb, sel_ref, msk_ref, s_ref):
    L = HG_CHUNK
    K = HG_DK
    x1 = jnp.log(lb)
    x2 = jnp.log1p(-lb) + _log_sigmoid(fr)
    logf = jnp.maximum(x1, x2) + _log1p_exp_neg(jnp.abs(x1 - x2))
    k = (1.0 - lb) * _sigmoid(-fr)
    ex = _dot_sel(sel_ref[...], logf)
    bcum = ex[0:L]
    after = ex[L:2 * L]
    heads = [slice(h * K, (h + 1) * K) for h in range(HG_HEADS)]

    def scaled(lev):
        e = jnp.exp(ex[(2 + lev) * L:(3 + lev) * L])
        return (q * e).astype(BF16), (k * e).astype(BF16)

    qs, ks = q.astype(BF16), k.astype(BF16)
    att = [None] * HG_HEADS
    for lev in range(HG_LEVELS, -1, -1):
        nxt = scaled(lev - 1) if lev > 0 else None
        for h, sl in enumerate(heads):
            term = msk_ref[lev] * _dot_nt(qs[:, sl], ks[:, sl])
            att[h] = term if att[h] is None else att[h] + term
        if nxt is not None:
            qs, ks = nxt
    vb = v.astype(BF16)
    q_in = (q * jnp.exp(bcum)).astype(BF16)
    k_w = (k * jnp.exp(after)).astype(BF16)
    e_end = jnp.exp(bcum[0:1] + after[0:1])
    outs = []
    for h, sl in enumerate(heads):
        st = s_ref[h]
        outs.append(_dot(att[h].astype(BF16), vb[:, sl]) + _dot_nt(q_in[:, sl], st.astype(BF16)))
        s_ref[h] = st * e_end[:, sl] + _dot_tn(vb[:, sl], k_w[:, sl])
    return jnp.concatenate(outs, axis=1)


def _hgrn_rows_kernel(q_ref, f_ref, v_ref, lb_ref, sel_ref, msk_ref, s0_ref, y_ref, sout_ref, s_scr,
                      *, nsteps, nsub, rev):
    c = pl.program_id(1)

    @pl.when(c == 0)
    def _():
        s_scr[...] = s0_ref[0]

    lb = lb_ref[...]
    for u in (range(nsub - 1, -1, -1) if rev else range(nsub)):
        rs = slice(u * HG_CHUNK, (u + 1) * HG_CHUNK)
        y_ref[0, rs, :] = _hgrn_chunk(q_ref[0, rs, :], f_ref[0, rs, :], v_ref[0, rs, :], lb, sel_ref, msk_ref, s_scr)

    @pl.when(c == nsteps - 1)
    def _():
        sout_ref[0] = s_scr[...]


def _hgrn_cols_kernel(q_ref, f_ref, v_ref, lb_ref, sel_ref, msk_ref, s0_ref, y_ref, sout_ref,
                      q_scr, f_scr, v_scr, y_scr, s_scr, *, nsteps, nsub, rev):
    c = pl.program_id(1)
    w = (nsteps - 1 - c) if rev else c
    wi = w % 8
    enter, leave = (7, 0) if rev else (0, 7)

    @pl.when(c == 0)
    def _():
        s_scr[...] = s0_ref[0]

    @pl.when(wi == enter)
    def _():
        for k in range(8):
            q_scr[k] = q_ref[0, :, 0, k, :]
            f_scr[k] = f_ref[0, :, 0, k, :]
            v_scr[k] = v_ref[0, :, 0, k, :]

    lb = lb_ref[...]
    for u in (range(nsub - 1, -1, -1) if rev else range(nsub)):
        rs = pl.ds(u * HG_CHUNK, HG_CHUNK)
        y_scr[wi, rs, :] = _hgrn_chunk(q_scr[wi, rs, :], f_scr[wi, rs, :], v_scr[wi, rs, :], lb, sel_ref, msk_ref,
                                       s_scr)

    @pl.when(wi == leave)
    def _():
        for k in range(8):
            y_ref[0, :, 0, k, :] = y_scr[k]

    @pl.when(c == nsteps - 1)
    def _():
        sout_ref[0] = s_scr[...]


def _hgrn_scan(uc, u, lb, s0, *, rev, col_major):
    sel, msk = _hgrn_consts(rev)
    s_shape = s0.shape
    const_specs = [pl.BlockSpec((1, HG_F), lambda i, c: (0, 0)),
                   pl.BlockSpec((HG_SEL_ROWS, HG_CHUNK), lambda i, c: (0, 0)),
                   pl.BlockSpec((HG_LEVELS + 1, HG_CHUNK, HG_CHUNK), lambda i, c: (0, 0, 0)),
                   pl.BlockSpec((1,) + s_shape[1:], lambda i, c: (i, 0, 0, 0))]
    s_out = pl.BlockSpec((1,) + s_shape[1:], lambda i, c: (i, 0, 0, 0))
    name = ("hgrn_cols" if col_major else "hgrn_rows") + ("_bwd" if rev else "_fwd")
    cols = ((COL_HQ - UC_COLS) // HG_F, (COL_HGF - UC_COLS) // HG_F + rev, COL_HGI // HG_F)
    if col_major:
        b, r, wg, _, _ = u.shape
        nsteps = wg * 8
        nsub = r // HG_CHUNK
        st = (lambda c: nsteps - 1 - c) if rev else (lambda c: c)
        blk = (1, r, 1, 8, HG_F)
        in_specs = [pl.BlockSpec(blk, lambda i, c, cb=cb: (i, 0, st(c) // 8, 0, cb)) for cb in cols]
        out_specs = [pl.BlockSpec(blk, lambda i, c: (i, 0, st(c) // 8, 0, 0)), s_out]
        out_shape = [jax.ShapeDtypeStruct((b, r, wg, 8, HG_D), F32), jax.ShapeDtypeStruct(s_shape, F32)]
        scratch = [pltpu.VMEM((8, r, HG_F), F32)] * 4 + [pltpu.VMEM(s_shape[1:], F32)]
        body = functools.partial(_hgrn_cols_kernel, nsteps=nsteps, nsub=nsub, rev=rev)
    else:
        b, t, _ = u.shape
        nsub = min(STEP_CHUNKS, t // HG_CHUNK)
        rows = nsub * HG_CHUNK
        nsteps = t // rows
        st = (lambda c: nsteps - 1 - c) if rev else (lambda c: c)
        in_specs = [pl.BlockSpec((1, rows, HG_F), lambda i, c, cb=cb: (i, st(c), cb)) for cb in cols]
        out_specs = [pl.BlockSpec((1, rows, HG_D), lambda i, c: (i, st(c), 0)), s_out]
        out_shape = [jax.ShapeDtypeStruct((b, t, HG_D), F32), jax.ShapeDtypeStruct(s_shape, F32)]
        scratch = [pltpu.VMEM(s_shape[1:], F32)]
        body = functools.partial(_hgrn_rows_kernel, nsteps=nsteps, nsub=nsub, rev=rev)
    return pl.pallas_call(
        body,
        grid=(b, nsteps),
        in_specs=in_specs + const_specs,
        out_specs=out_specs,
        out_shape=out_shape,
        scratch_shapes=scratch,
        compiler_params=_cparams(2),
        name=name,
    )(u, u, uc, lb, sel, msk, s0)


def _pad_rows(a, rows):
    return jnp.concatenate([a, jnp.zeros((rows - a.shape[0],) + a.shape[1:], a.dtype)], axis=0)


def _lane_vec(parts, width=128):
    out = jnp.zeros((width,), F32)
    for off, vals in parts:
        out = lax.dynamic_update_slice(out, vals.astype(F32), (off,))
    return out


def _permute_w_in(w_in):
    offs = np.cumsum([0, SSD_D, SSD_XBC, 2 * SSD_HEADS, 2 * ML_D, ML_D, ML_D, 2 * ML_HEADS, 2 * ML_HEADS,
                      HG_F, 2 * HG_F, HG_D, HG_D])
    z, xbc, dt, mqk, mv, mo, ig, fg, hq, hf, hi, hg = [w_in[..., offs[i]:offs[i + 1]] for i in range(12)]
    pad = jnp.zeros(w_in.shape[:-1] + (512 - (dt.shape[-1] + ig.shape[-1] + fg.shape[-1]),), w_in.dtype)
    parts = [mqk, hi, xbc, z, hf, mv, mo, hq, hg, dt, ig, fg, pad]
    return jnp.concatenate([p.astype(BF16) for p in parts], axis=-1)


def kernel(x, c, ctx, c_ctx, ada_w, ada_b, norm_w, ffn_w_gate, ffn_w_up, ffn_w_down, w_in, w_out,
           ssd_conv_w, ssd_conv_b, ssd_a_log, ssd_dt_bias, ssd_d, ssd_norm_w, ml_conv_w, ml_conv_b,
           ml_ig_b, ml_fg_b, ml_norm_w, hg_lb_logits, hg_norm_w, final_norm_w):
    bsz, seq, d = x.shape
    ctx_len = ctx.shape[1]
    depth = ada_w.shape[0]
    rows = seq // GRID_W
    n_stream = bsz + 1

    pr = jax.nn.softmax(hg_lb_logits.astype(F32), axis=0)
    cum = jnp.cumsum(pr, axis=0)
    lb_all = cum - cum[0]

    cvec = _pad_rows(jnp.concatenate([c, c_ctx[None, :]], axis=0), 8)
    mods = _ada_mods(cvec, ada_w, ada_b)

    xl = x.reshape(bsz * seq, d)
    xc = ctx.reshape(bsz * ctx_len, d)
    tm_l = 512
    tm_c = min(512, bsz * ctx_len)
    lat_stream = lambda tm: (lambda i: i // (seq // tm))
    ctx_stream = lambda tm: (lambda i: bsz)
    fw = final_norm_w.reshape(1, d).astype(F32)
    zeros = lambda shape: jnp.zeros(shape, F32)
    wg, wu, wd = (w.astype(BF16) for w in (ffn_w_gate, ffn_w_up, ffn_w_down))
    wp = _permute_w_in(w_in)
    wo = w_out.astype(BF16)

    for l in range(depth):
        need_ctx = l < depth - 1
        mod = mods[l, :n_stream].reshape(n_stream, 3, 3, d)

        def pack(sub):
            nw = jnp.broadcast_to(norm_w[l, sub][None, None, :], (n_stream, 1, d))
            return jnp.concatenate([mod[:, sub], nw, zeros((n_stream, 4, d))], axis=1)

        pk0 = pack(0)
        xl = _ffn(xl, pk0, lat_stream(tm_l), wg, wu, wd, (l, 0), fw, tm=tm_l, tf=512)
        xc = _ffn(xc, pk0, ctx_stream(tm_c), wg, wu, wd, (l, 0), fw, tm=tm_c, tf=512)

        pk1 = pack(1)
        s_cw = _pad_rows(ssd_conv_w[l].astype(F32), 8)
        s_cb = ssd_conv_b[l].reshape(1, SSD_XBC).astype(F32)
        m_cw = _pad_rows(ml_conv_w[l].astype(F32), 8)
        m_cb = ml_conv_b[l].reshape(1, 2 * ML_D).astype(F32)
        ucl, ul = _proj(xl, pk1, lat_stream(tm_l), wp, l, m_cw, m_cb, s_cw, s_cb, tm=tm_l, tn=1536, seq=seq)
        ucc, uc = _proj(xc, pk1, ctx_stream(ctx_len), wp, l, m_cw, m_cb, s_cw, s_cb, tm=ctx_len, tn=1536,
                        seq=ctx_len)
        ucl, ul = (a.reshape(bsz, seq, a.shape[-1]) for a in (ucl, ul))
        ucc, uc = (a.reshape(bsz, ctx_len, a.shape[-1]) for a in (ucc, uc))

        a_neg = -jnp.exp(ssd_a_log[l].astype(F32))
        dv = jnp.repeat(ssd_d[l].astype(F32), SSD_HEAD_DIM).reshape(1, SSD_D)
        hp = jnp.stack([_pad_rows(jnp.stack([_lane_vec([(SM_DT + SSD_HEADS * r, ssd_dt_bias[l, r])]),
                                             _lane_vec([(SM_DT + SSD_HEADS * r, a_neg[r])])]), 8)
                        for r in (0, 1)])
        h0 = zeros((2, bsz, SSD_HEADS // 2, SSD_STATE, 2 * SSD_HEAD_DIM))
        ysc_f, ysc_b, hc = _ssd_scan(ucc, uc, hp, dv, h0)
        ysl_f, ysl_b, _ = _ssd_scan(ucl, ul, hp, dv, hc)

        gb =jnp.stack([_pad_rows(jnp.stack([_lane_vec([(SM_IG + ML_HEADS * r, ml_ig_b[l, r])]),
                                             _lane_vec([(SM_FG + ML_HEADS * r, ml_fg_b[l, r])])]), 8)
                        for r in (0, 1)])
        c0 = zeros((2, bsz, ML_HEADS, ML_HEAD_DIM, 2 * ML_HEAD_DIM))
        m0 = zeros((2, bsz, ML_HEADS, 8, 128))
        ymc_f, ymc_b, cc, mc = _mlstm_scan(ucc, uc, gb, c0, m0)
        yml_f, yml_b, _, _ = _mlstm_scan(ucl, ul, gb, cc, mc)

        lb = lb_all[l].reshape(1, HG_F)
        grid5 = lambda a: a.reshape(bsz, rows, GRID_W // 8, 8, a.shape[-1])
        yh_l, yh_c = [], []
        for rev in (0, 1):
            s0 = zeros((bsz, HG_HEADS, HG_DV, HG_DK))
            yc_, sc = _hgrn_scan(ucc, uc, lb, s0, rev=rev, col_major=False)
            yl_, _ = _hgrn_scan(grid5(ucl), grid5(ul), lb, sc, rev=rev, col_major=True)
            yh_c.append(yc_.reshape(bsz * ctx_len, HG_D))
            yh_l.append(yl_.reshape(bsz * seq, HG_D))

        nw = jnp.concatenate([ssd_norm_w[l], ml_norm_w[l], hg_norm_w[l]]).reshape(1, d).astype(F32)
        flat = lambda a: a.reshape(-1, a.shape[-1])
        xl = _merge(xl, pk1, lat_stream(256), [flat(ysl_f), flat(ysl_b)], [flat(yml_f), flat(yml_b)], yh_l,
                    flat(ul), nw, wo, l, tm=256)

        pk2 = pack(2)
        xl = _ffn(xl, pk2, lat_stream(tm_l), wg, wu, wd, (l, 1), fw, tm=tm_l, tf=512, final=not need_ctx)
        if need_ctx:
            xc = _merge(xc, pk1, ctx_stream(256), [flat(ysc_f), flat(ysc_b)], [flat(ymc_f), flat(ymc_b)], yh_c,
                        flat(uc), nw, wo, l, tm=256)
            xc = _ffn(xc, pk2, ctx_stream(tm_c), wg, wu, wd, (l, 1), fw, tm=tm_c, tf=512)

    return xl.reshape(bsz, seq, d)
```

```python
import functools

import numpy as np
import jax
import jax.numpy as jnp
from jax import lax
from jax.experimental import pallas as pl
from jax.experimental.pallas import tpu as pltpu

F32 = jnp.float32
BF16 = jnp.bfloat16

GRID_W = 64
FFN_HALF = 0.5
EPS = 1e-6
CHUNK = 128
HG_CHUNK = 128
CONV_W = 5

SSD_HEADS = 16
SSD_HEAD_DIM = 64
SSD_GROUPS = 2
SSD_STATE = 128
ML_HEADS = 4
ML_HEAD_DIM = 128
HG_HEADS = 4
HG_DK = 128
HG_DV = 128

SSD_D = SSD_HEADS * SSD_HEAD_DIM
SSD_XBC = SSD_D + 2 * SSD_GROUPS * SSD_STATE
ML_D = ML_HEADS * ML_HEAD_DIM
HG_F = HG_HEADS * HG_DK
HG_D = HG_HEADS * HG_DV

COL_MQK = 0
COL_HGI = 1024
COL_XBC = 1536
COL_Z = 3072
COL_HGF = 4096
COL_MV = 5120
COL_MO = 5632
COL_HQ = 6144
COL_HGG = 6656
COL_SM = 7168
U_COLS = 7680
UC_COLS = 3072
SM_DT = 0
SM_IG = 2 * SSD_HEADS
SM_FG = 2 * SSD_HEADS + 2 * ML_HEADS

NEG = -1e30
VMEM_LIMIT_BYTES = 56 * 1024 * 1024


def _cparams(n_axes):
    return pltpu.CompilerParams(dimension_semantics=("arbitrary",) * n_axes,
                                vmem_limit_bytes=VMEM_LIMIT_BYTES)


def _sigmoid(x):
    return 1.0 / (1.0 + jnp.exp(-x))


def _silu(x):
    return x * _sigmoid(x)


def _log1p_exp_neg(y):
    return jnp.log(1.0 + jnp.exp(-y))


def _log_sigmoid(x):
    return jnp.minimum(x, 0.0) - _log1p_exp_neg(jnp.abs(x))


def _softplus(x):
    return jnp.maximum(x, 0.0) + jnp.log1p(jnp.exp(-jnp.abs(x)))


def _dot(a, b):
    return jnp.dot(a, b, preferred_element_type=F32)


def _dot_nt(a, b):
    return lax.dot_general(a, b, (((1,), (1,)), ((), ())), preferred_element_type=F32)


def _dot_tn(a, b):
    return lax.dot_general(a, b, (((0,), (0,)), ((), ())), preferred_element_type=F32)


def _dot_sel(sel_bf16, x):
    hi = x.astype(BF16)
    r1 = x - hi.astype(F32)
    mid = r1.astype(BF16)
    lo = (r1 - mid.astype(F32)).astype(BF16)
    return _dot(sel_bf16, hi) + _dot(sel_bf16, mid) + _dot(sel_bf16, lo)


def _norm_mod(x, pk):
    ms = jnp.mean(x * x, axis=-1, keepdims=True)
    xn = (x * lax.rsqrt(ms + EPS)) * pk[3:4]
    return xn * (1.0 + pk[1:2]) + pk[0:1]


def _ada_kernel(c_ref, w_ref, b_ref, o_ref):
    c = _silu(c_ref[...]).astype(BF16)
    o_ref[0] = _dot(c, w_ref[0].astype(BF16)) + b_ref[0]


def _ada_mods(cvec, ada_w, ada_b):
    depth, d, n = ada_w.shape
    tn = 1024
    return pl.pallas_call(
        _ada_kernel,
        grid=(depth, n // tn),
        in_specs=[pl.BlockSpec((8, d), lambda l, j: (0, 0)),
                  pl.BlockSpec((1, d, tn), lambda l, j: (l, 0, j)),
                  pl.BlockSpec((1, 1, tn), lambda l, j: (l, 0, j))],
        out_specs=pl.BlockSpec((1, 8, tn), lambda l, j: (l, 0, j)),
        out_shape=jax.ShapeDtypeStruct((depth, 8, n), F32),
        compiler_params=_cparams(2),
        name="ada_mods",
    )(cvec, ada_w, ada_b.reshape(depth, 1, n))


def _ffn_kernel(x_ref, pk_ref, wg_ref, wu_ref, wd_ref, fw_ref, o_ref, h_scr, acc_scr, *, nf, final):
    f = pl.program_id(1)

    @pl.when(f == 0)
    def _():
        h_scr[...] = _norm_mod(x_ref[...], pk_ref[0]).astype(BF16)
        acc_scr[...] = jnp.zeros_like(acc_scr)

    h = h_scr[...]
    g = _dot(h, wg_ref[...])
    u = _dot(h, wu_ref[...])
    a = (_silu(g) * u).astype(BF16)
    acc_scr[...] += _dot(a, wd_ref[...])

    @pl.when(f == nf - 1)
    def _():
        y = x_ref[...] + (FFN_HALF * pk_ref[0][2:3]) * acc_scr[...]
        if final:
            ms = jnp.mean(y * y, axis=-1, keepdims=True)
            y = (y * lax.rsqrt(ms + EPS)) * fw_ref[...]
        o_ref[...] = y


def _ffn(x, pack, stream_of_tile, wg, wu, wd, ls, fw, *, tm, tf, final=False):
    m, d = x.shape
    l, s = ls
    nf = wg.shape[-1] // tf
    return pl.pallas_call(
        functools.partial(_ffn_kernel, nf=nf, final=final),
        grid=(m // tm, nf),
        in_specs=[pl.BlockSpec((tm, d), lambda i, f: (i, 0)),
                  pl.BlockSpec((1, 8, d), lambda i, f: (stream_of_tile(i), 0, 0)),
                  pl.BlockSpec((None, None, d, tf), lambda i, f: (l, s, 0, f)),
                  pl.BlockSpec((None, None, d, tf), lambda i, f: (l, s, 0, f)),
                  pl.BlockSpec((None, None, tf, d), lambda i, f: (l, s, f, 0)),
                  pl.BlockSpec((1, d), lambda i, f: (0, 0))],
        out_specs=pl.BlockSpec((tm, d), lambda i, f: (i, 0)),
        out_shape=jax.ShapeDtypeStruct((m, d), F32),
        scratch_shapes=[pltpu.VMEM((tm, d), BF16), pltpu.VMEM((tm, d), F32)],
        compiler_params=_cparams(2),
        name="ffn",
    )(x, pack, wg, wu, wd, fw)


HALO = 16


def _proj_kernel(x_ref, xp_ref, xn_ref, pk_ref, w_ref, cwa_ref, cba_ref, cwb_ref, cbb_ref, oc_ref, o_ref,
                 h_scr, raw_scr, *, tm, tiles_per_seq):
    i = pl.program_id(0)
    j = pl.program_id(1)
    tn = o_ref.shape[1]

    def conv_tile(slot, cw_ref, cb_ref, ncols):
        keep_p = (i % tiles_per_seq != 0).astype(F32)
        keep_n = (i % tiles_per_seq != tiles_per_seq - 1).astype(F32)
        rb, cb = 64, 256
        for c0 in range(0, ncols, cb):
            w8 = [jnp.broadcast_to(cw_ref[t:t + 1, c0:c0 + cb], (8, cb))[None] for t in range(CONV_W)]
            b8 = jnp.broadcast_to(cb_ref[:, c0:c0 + cb], (8, cb))[None]
            for r0 in range(0, tm, rb):
                blk = raw_scr[slot, HALO + r0 - 8:HALO + r0 + rb + 8, c0:c0 + cb]
                if r0 == 0:
                    blk = jnp.concatenate([blk[:8] * keep_p, blk[8:]], axis=0)
                if r0 == tm - rb:
                    blk = jnp.concatenate([blk[:rb + 8], blk[rb + 8:] * keep_n], axis=0)
                acc = b8
                for t in range(CONV_W):
                    sh = blk if t == CONV_W // 2 else pltpu.roll(blk, (CONV_W // 2 - t) % (rb + 16), axis=0)
                    acc = acc + _tiles(sh[8:8 + rb]) * w8[t]
                oc_ref[r0:r0 + rb, c0:c0 + cb] = _silu(acc).reshape(rb, cb)
        if ncols < tn:
            oc_ref[:, ncols:] = raw_scr[slot, HALO:HALO + tm, ncols:]

    @pl.when(j == 0)
    def _():
        pk = pk_ref[0]
        h_scr[0:HALO, :] = _norm_mod(xp_ref[...], pk).astype(BF16)
        h_scr[HALO:HALO + tm, :] = _norm_mod(x_ref[...], pk).astype(BF16)
        h_scr[HALO + tm:, :] = _norm_mod(xn_ref[...], pk).astype(BF16)
        raw_scr[0] = _dot(h_scr[...], w_ref[...])

    @pl.when(j == 1)
    def _():
        conv_tile(0, cwa_ref, cba_ref, 2 * ML_D)
        raw_scr[1] = _dot(h_scr[...], w_ref[...])

    @pl.when(j == 2)
    def _():
        conv_tile(1, cwb_ref, cbb_ref, SSD_XBC)
        o_ref[...] = _dot(h_scr[HALO:HALO + tm, :], w_ref[...])

    @pl.when(j > 2)
    def _():
        o_ref[...] = _dot(h_scr[HALO:HALO + tm, :], w_ref[...])


def _proj(x, pack, stream_of_tile, w, layer, cwa, cba, cwb, cbb, *, tm, tn, seq):
    m, d = x.shape
    assert (COL_MQK, COL_XBC, UC_COLS) == (0, tn, 2 * tn)
    per = tm // HALO
    last = m // HALO - 1
    nconv = UC_COLS // tn
    return pl.pallas_call(
        functools.partial(_proj_kernel, tm=tm, tiles_per_seq=seq // tm),
        grid=(m // tm, U_COLS // tn),
        in_specs=[pl.BlockSpec((tm, d), lambda i, j: (i, 0)),
                  pl.BlockSpec((HALO, d), lambda i, j: (jnp.maximum(i * per - 1, 0), 0)),
                  pl.BlockSpec((HALO, d), lambda i, j: (jnp.minimum((i + 1) * per, last), 0)),
                  pl.BlockSpec((1, 8, d), lambda i, j: (stream_of_tile(i), 0, 0)),
                  pl.BlockSpec((None, d, tn), lambda i, j: (layer, 0, j)),
                  pl.BlockSpec((8, 2 * ML_D), lambda i, j: (0, 0)),
                  pl.BlockSpec((1, 2 * ML_D), lambda i, j: (0, 0)),
                  pl.BlockSpec((8, SSD_XBC), lambda i, j: (0, 0)),
                  pl.BlockSpec((1, SSD_XBC), lambda i, j: (0, 0))],
        out_specs=[pl.BlockSpec((tm, tn), lambda i, j: (i, jnp.clip(j - 1, 0, nconv - 1))),
                   pl.BlockSpec((tm, tn), lambda i, j: (i, jnp.maximum(j - nconv, 0)))],
        out_shape=[jax.ShapeDtypeStruct((m, UC_COLS), F32), jax.ShapeDtypeStruct((m, U_COLS - UC_COLS), F32)],
        scratch_shapes=[pltpu.VMEM((tm + 2 * HALO, d), BF16), pltpu.VMEM((2, tm + 2 * HALO, tn), F32)],
        compiler_params=_cparams(2),
        name="in_proj",
    )(x, x, x, pack, w, cwa, cba, cwb, cbb)


def _group_norm(y, w, group):
    outs = []
    for g in range(y.shape[1] // group):
        yg = y[:, g * group:(g + 1) * group]
        ms = jnp.mean(yg * yg, axis=-1, keepdims=True)
        outs.append(yg * lax.rsqrt(ms + EPS))
    return jnp.concatenate(outs, axis=1) * w


def _merge_kernel(x_ref, pk_ref, sf_ref, sb_ref, mf_ref, mb_ref, hf_ref, hb_ref, z_ref, mo_ref, hg_ref,
                  nw_ref, wo_ref, o_ref):
    nw = nw_ref[...]
    ys = (sf_ref[...] + sb_ref[...]) * _silu(z_ref[...])
    ys = _group_norm(ys, nw[:, :SSD_D], SSD_D // SSD_GROUPS)
    ym = _group_norm(mf_ref[...] + mb_ref[...], nw[:, SSD_D:SSD_D + ML_D], ML_HEAD_DIM) * _sigmoid(mo_ref[...])
    yh = _group_norm(hf_ref[...] + hb_ref[...], nw[:, SSD_D + ML_D:], HG_DV) * _silu(hg_ref[...])
    y = jnp.concatenate([ys, ym, yh], axis=1).astype(BF16)
    o_ref[...] = x_ref[...] + pk_ref[0][2:3] * _dot(y, wo_ref[...])


def _merge(x, pack, stream_of_tile, ys, ym, yh, u, nw, wo, layer, *, tm):
    m, d = x.shape

    def rows(width, colblk):
        return pl.BlockSpec((tm, width), lambda i: (i, colblk))

    return pl.pallas_call(
        _merge_kernel,
        grid=(m // tm,),
        in_specs=[rows(d, 0),
                  pl.BlockSpec((1, 8, d), lambda i: (stream_of_tile(i), 0, 0)),
                  rows(SSD_D, 0), rows(SSD_D, 0), rows(ML_D, 0), rows(ML_D, 0), rows(HG_D, 0), rows(HG_D, 0),
                  rows(SSD_D, (COL_Z - UC_COLS) // SSD_D), rows(ML_D, (COL_MO - UC_COLS) // ML_D),
                  rows(HG_D, (COL_HGG - UC_COLS) // HG_D),
                  pl.BlockSpec((1, d), lambda i: (0, 0)),
                  pl.BlockSpec((None, d, d), lambda i: (layer, 0, 0))],
        out_specs=rows(d, 0),
        out_shape=jax.ShapeDtypeStruct((m, d), F32),
        compiler_params=_cparams(1),
        name="merge_out",
    )(x, pack, ys[0], ys[1], ym[0], ym[1], yh[0], yh[1], u, u, u, nw, wo)


STEP_CHUNKS = 4


def _scan_tri(length, rev):
    r = lax.broadcasted_iota(jnp.int32, (length, length), 0)
    c = lax.broadcasted_iota(jnp.int32, (length, length), 1)
    return (c >= r) if rev else (c <= r)


def _scan_tri3(length, rev):
    hi = lax.broadcasted_iota(jnp.int32, (length // 8, 8, length), 0)
    lo = lax.broadcasted_iota(jnp.int32, (length // 8, 8, length), 1)
    c = lax.broadcasted_iota(jnp.int32, (length // 8, 8, length), 2)
    r = hi * 8 + lo
    return (c >= r) if rev else (c <= r)


def _tiles(x):
    return x.reshape(x.shape[0] // 8, 8, x.shape[1])


def _row8(x, i):
    return jnp.broadcast_to(x[i:i + 1, :], (8, x.shape[1]))[None]


def _ssd_chunk(xbc, raw, hp, dv_ref, state, y_ref, rs, rev):
    L = CHUNK
    P = SSD_HEAD_DIM
    xs = xbc[:, :SSD_D]
    bm = xbc[:, SSD_D:SSD_D + SSD_GROUPS * SSD_STATE]
    cm = xbc[:, SSD_D + SSD_GROUPS * SSD_STATE:]

    off = SM_DT + SSD_HEADS * rev
    dt = _softplus(raw + hp[0:1, :])
    la = dt * hp[1:2, :]
    a = _dot_sel(_scan_tri(L, rev).astype(BF16), la)
    mask = _scan_tri3(L, rev)
    a_end = a[0:1, :] if rev else a[L - 1:L, :]
    eend = jnp.exp(a_end)
    a_t = a.T
    dt_t = dt.T
    wf_t = (jnp.exp(a_end - a) * dt).T
    lane = lax.broadcasted_iota(jnp.int32, (1, 2 * P), 1)
    first = lane < P
    new_state = []

    for g in range(SSD_GROUPS):
        cg = cm[:, g * SSD_STATE:(g + 1) * SSD_STATE]
        bg = bm[:, g * SSD_STATE:(g + 1) * SSD_STATE]
        cb = _tiles(_dot_nt(cg.astype(BF16), bg.astype(BF16)))
        cg3 = _tiles(cg)
        bg_t = _tiles(bg.T)
        for pp in range(SSD_HEADS // SSD_GROUPS // 2):
            pair = g * (SSD_HEADS // SSD_GROUPS // 2) + pp
            xp = xs[:, pair * 2 * P:(pair + 1) * 2 * P]
            hpair = state[pair]
            lhs, rhs, blhs, xms, ends = [], [], [], [], []
            for k in range(2):
                ln = off + 2 * pair + k
                sel = first if k == 0 else jnp.logical_not(first)
                acol = _tiles(jnp.broadcast_to(a[:, ln:ln + 1], (L, L)))
                dec = jnp.exp(jnp.where(mask, acol - _row8(a_t, ln), NEG))
                lhs += [(cb * dec * _row8(dt_t, ln)).reshape(L, L), (cg3 * jnp.exp(acol)).reshape(L, SSD_STATE)]
                xm = jnp.where(sel, xp, 0.0)
                rhs += [xm, jnp.where(sel, hpair, 0.0)]
                blhs.append((bg_t * _row8(wf_t, ln)).reshape(SSD_STATE, L))
                xms.append(xm)
                ends.append(eend[:, ln:ln + 1])
            xcat = jnp.concatenate(xms, axis=0).astype(BF16)
            y = _dot(jnp.concatenate(lhs, axis=1).astype(BF16), jnp.concatenate(rhs, axis=0).astype(BF16))
            if not rev:
                y = y + dv_ref[:, pair * 2 * P:(pair + 1) * 2 * P] * xp
            y_ref[0, rs, pair * 2 * P:(pair + 1) * 2 * P] = y
            upd = _dot(jnp.concatenate(blhs, axis=1).astype(BF16), xcat)
            new_state.append(hpair * jnp.where(first, ends[0], ends[1]) + upd)
    return new_state


def _ssd_kernel(xf_ref, sf_ref, xb_ref, sb_ref, hp_ref, dv_ref, h0_ref, yf_ref, yb_ref, hout_ref, h_scr,
                *, nsteps, nsub):
    c = pl.program_id(1)

    @pl.when(c == 0)
    def _():
        h_scr[...] = h0_ref[:, 0]

    npair = SSD_HEADS // 2
    states = [[h_scr[d, p] for p in range(npair)] for d in range(2)]
    streams = ((xf_ref, sf_ref, yf_ref), (xb_ref, sb_ref, yb_ref))
    for d, (x_ref, s_ref, y_ref) in enumerate(streams):
        subs = range(nsub - 1, -1, -1) if d else range(nsub)
        for u in subs:
            rs = slice(u * CHUNK, (u + 1) * CHUNK)
            states[d] = _ssd_chunk(x_ref[0, rs, :], s_ref[0, rs, 0:128], hp_ref[d], dv_ref, states[d], y_ref, rs, d)
    for d in range(2):
        for p in range(npair):
            h_scr[d, p] = states[d][p]

    @pl.when(c == nsteps - 1)
    def _():
        hout_ref[:, 0] = h_scr[...]


def _ssd_scan(uc3, up3, hp, dv, h0):
    b, t, _ = uc3.shape
    nsub = min(STEP_CHUNKS, t // CHUNK)
    rows = nsub * CHUNK
    nsteps = t // rows
    fwd = lambda c: c
    bwd = lambda c: nsteps - 1 - c
    st_shape = h0.shape

    def stream_specs(step_of):
        return [pl.BlockSpec((1, rows, SSD_XBC), lambda i, c: (i, step_of(c), COL_XBC // SSD_XBC)),
                pl.BlockSpec((1, rows, 512), lambda i, c: (i, step_of(c), (COL_SM - UC_COLS) // 512))]

    st_spec = pl.BlockSpec((2, 1) + st_shape[2:], lambda i, c: (0, i, 0, 0, 0))
    return pl.pallas_call(
        functools.partial(_ssd_kernel, nsteps=nsteps, nsub=nsub),
        grid=(b, nsteps),
        in_specs=stream_specs(fwd) + stream_specs(bwd) + [
            pl.BlockSpec((2, 8, 128), lambda i, c: (0, 0, 0)),
            pl.BlockSpec((1, SSD_D), lambda i, c: (0, 0)),
            st_spec],
        out_specs=[pl.BlockSpec((1, rows, SSD_D), lambda i, c: (i, fwd(c), 0)),
                   pl.BlockSpec((1, rows, SSD_D), lambda i, c: (i, bwd(c), 0)),
                   st_spec],
        out_shape=[jax.ShapeDtypeStruct((b, t, SSD_D), F32), jax.ShapeDtypeStruct((b, t, SSD_D), F32),
                   jax.ShapeDtypeStruct(st_shape, F32)],
        scratch_shapes=[pltpu.VMEM((2,) + st_shape[2:], F32)],
        compiler_params=_cparams(2),
        name="ssd_scan",
    )(uc3, up3, uc3, up3, hp, dv, h0)


def _mlstm_prelude(raw, gb, rev):
    logi = raw + gb[0:1, :]
    logf = _log_sigmoid(raw + gb[1:2, :])
    bcum = _dot_sel(_scan_tri(CHUNK, rev).astype(BF16), logf)
    return bcum, bcum.T, logi.T


def _mlstm_head(qh, kh, vh, pre, h, state, rev):
    L = CHUNK
    K = ML_HEAD_DIM
    bcum, b_t, logi_t = pre
    li = SM_IG + ML_HEADS * rev + h
    lf = SM_FG + ML_HEADS * rev + h
    end = 0 if rev else L - 1
    mask = _scan_tri3(L, rev)
    kh = kh * K ** -0.5
    vaug = jnp.concatenate([vh.astype(BF16), jnp.ones((L, K), BF16)], axis=1)
    c_old, m_old = state
    brow = b_t[lf:lf + 1, :]
    irow = logi_t[li:li + 1, :]
    bcol = _tiles(jnp.broadcast_to(bcum[:, lf:lf + 1], (L, L)))
    dm = jnp.where(mask, bcol - jnp.broadcast_to(brow - irow, (8, L))[None], NEG)
    gfull = bcol + m_old
    mt = jnp.maximum(gfull, jnp.max(dm, axis=2, keepdims=True))
    qkm = _tiles(_dot_nt(qh.astype(BF16), kh.astype(BF16))) * jnp.exp(dm - mt)
    qin = _tiles(qh) * jnp.exp(gfull - mt)
    lhs = jnp.concatenate([qkm.reshape(L, L), qin.reshape(L, K)], axis=1).astype(BF16)
    rhs = jnp.concatenate([vaug, c_old.astype(BF16)], axis=0)
    s = _dot(lhs, rhs)
    den = jnp.maximum(jnp.abs(s[:, K:]), jnp.exp(-mt).reshape(L, L))
    bl = brow[:, end:end + 1]
    ds = bl - brow + irow
    m_new = jnp.maximum(bl + m_old, jnp.max(ds, axis=1, keepdims=True))
    ws = jnp.exp(ds - m_new)
    keep = jnp.exp(bl + m_old - m_new)
    c_new = c_old * keep + _dot((kh.T * ws).astype(BF16), vaug)
    return s[:, :K] / den, (c_new, m_new)


def _mlstm_kernel(qf_ref, vf_ref, sf_ref, qb_ref, vb_ref, sb_ref, gb_ref, c0_ref, m0_ref,
                  yf_ref, yb_ref, cout_ref, mout_ref, c_scr, m_scr, *, nsteps, nsub):
    c = pl.program_id(1)
    K = ML_HEAD_DIM

    @pl.when(c == 0)
    def _():
        c_scr[...] = c0_ref[:, 0]
        m_scr[...] = m0_ref[:, 0]

    states = [[(c_scr[d, h], m_scr[d, h][0:1, 0:1]) for h in range(ML_HEADS)] for d in range(2)]
    streams = ((qf_ref, vf_ref, sf_ref, yf_ref), (qb_ref, vb_ref, sb_ref, yb_ref))
    for k in range(nsub):
        ctx = []
        for d, (q_ref, v_ref, s_ref, y_ref) in enumerate(streams):
            u = nsub - 1 - k if d else k
            rs = slice(u * CHUNK, (u + 1) * CHUNK)
            ctx.append((rs, _mlstm_prelude(s_ref[0, rs, 0:128], gb_ref[d], d)))
        for h in range(ML_HEADS):
            for d, (q_ref, v_ref, s_ref, y_ref) in enumerate(streams):
                rs, pre = ctx[d]
                y, states[d][h] = _mlstm_head(q_ref[0, rs, h * K:(h + 1) * K],
                                              q_ref[0, rs, ML_D + h * K:ML_D + (h + 1) * K],
                                              v_ref[0, rs, h * K:(h + 1) * K], pre, h, states[d][h], d)
                y_ref[0, rs, h * K:(h + 1) * K] = y
    for d in range(2):
        for h in range(ML_HEADS):
            c_scr[d, h] = states[d][h][0]
            m_scr[d, h] = jnp.broadcast_to(states[d][h][1], (8, 128))

    @pl.when(c == nsteps - 1)
    def _():
        cout_ref[:, 0] = c_scr[...]
        mout_ref[:, 0] = m_scr[...]


def _mlstm_scan(uc3, up3, gb, c0, m0):
    b, t, _ = uc3.shape
    nsub = min(STEP_CHUNKS, t // CHUNK)
    rows = nsub * CHUNK
    nsteps = t // rows
    fwd = lambda c: c
    bwd = lambda c: nsteps - 1 - c

    def stream_specs(step_of):
        return [pl.BlockSpec((1, rows, 2 * ML_D), lambda i, c: (i, step_of(c), COL_MQK // (2 * ML_D))),
                pl.BlockSpec((1, rows, ML_D), lambda i, c: (i, step_of(c), (COL_MV - UC_COLS) // ML_D)),
                pl.BlockSpec((1, rows, 512), lambda i, c: (i, step_of(c), (COL_SM - UC_COLS) // 512))]

    c_spec = pl.BlockSpec((2, 1) + c0.shape[2:], lambda i, c: (0, i, 0, 0, 0))
    m_spec = pl.BlockSpec((2, 1) + m0.shape[2:], lambda i, c: (0, i, 0, 0, 0))
    return pl.pallas_call(
        functools.partial(_mlstm_kernel, nsteps=nsteps, nsub=nsub),
        grid=(b, nsteps),
        in_specs=stream_specs(fwd) + stream_specs(bwd) + [
            pl.BlockSpec((2, 8, 128), lambda i, c: (0, 0, 0)),
            c_spec, m_spec],
        out_specs=[pl.BlockSpec((1, rows, ML_D), lambda i, c: (i, fwd(c), 0)),
                   pl.BlockSpec((1, rows, ML_D), lambda i, c: (i, bwd(c), 0)),
                   c_spec, m_spec],
        out_shape=[jax.ShapeDtypeStruct((b, t, ML_D), F32), jax.ShapeDtypeStruct((b, t, ML_D), F32),
                   jax.ShapeDtypeStruct(c0.shape, F32), jax.ShapeDtypeStruct(m0.shape, F32)],
        scratch_shapes=[pltpu.VMEM((2,) + c0.shape[2:], F32), pltpu.VMEM((2,) + m0.shape[2:], F32)],
        compiler_params=_cparams(2),
        name="mlstm_scan",
    )(uc3, up3, up3, uc3, up3, up3, gb, c0, m0)


HG_LEVELS = 7
HG_SEL_LEVELS = 3
HG_SEL_ROWS = (2 + HG_SEL_LEVELS) * HG_CHUNK


def _hgrn_consts(rev):
    L = HG_CHUNK
    pos = np.arange(L)
    p = pos[::-1] if rev else pos
    cum = (p[None, :] <= p[:, None])
    after = (p[None, :] > p[:, None])
    sels = [cum, after]
    masks = []
    for lev in range(HG_LEVELS):
        m = 1 << lev
        blk = p // (2 * m)
        late = (p % (2 * m)) >= m
        ref = blk * 2 * m + m - 1
        pj = p[None, :]
        sel = np.where(late[:, None], (pj > ref[:, None]) & (pj <= p[:, None]),
                       (pj > p[:, None]) & (pj <= ref[:, None]))
        if lev < HG_SEL_LEVELS:
            sels.append(sel)
        masks.append((blk[:, None] == blk[None, :]) & late[:, None] & (~late)[None, :])
    masks.append(np.eye(L, dtype=bool))
    sel_all = np.concatenate(sels, axis=0).astype(np.float32)
    mask_all = np.stack(masks, axis=0).astype(np.float32)
    return jnp.asarray(sel_all, BF16), jnp.asarray(mask_all, F32)


def _level_exponent(bc3, lev, rev):
    m8 = (1 << lev) // 8
    refs = {}
    tiles = []
    for tau in range(bc3.shape[0]):
        blk, w = divmod(tau, 2 * m8)
        upper = w >= m8
        if blk not in refs:
            row = bc3[blk * 2 * m8 + m8, 0:1, :] if rev else bc3[blk * 2 * m8 + m8 - 1, 7:8, :]
            refs[blk] = jnp.broadcast_to(row, bc3.shape[1:])
        late = (not upper) if rev else upper
        tiles.append(bc3[tau] - refs[blk] if late else refs[blk] - bc3[tau])
    return jnp.stack(tiles, axis=0).reshape(bc3.shape[0] * 8, bc3.shape[2])


def _hgrn_chunk(q, fr, v, lb, sel_ref, msk_ref, s_ref, rev):
    L = HG_CHUNK
    K = HG_DK
    x1 = jnp.log(lb)
    x2 = jnp.log1p(-lb) + _log_sigmoid(fr)
    logf = jnp.maximum(x1, x2) + _log1p_exp_neg(jnp.abs(x1 - x2))
    k = (1.0 - lb) * _sigmoid(-fr)
    ex = _dot_sel(sel_ref[...], logf)
    bcum = ex[0:L]
    after = ex[L:2 * L]
    heads = [slice(h * K, (h + 1) * K) for h in range(HG_HEADS)]

    bc3 = _tiles(bcum)

    def scaled(lev):
        if lev < HG_SEL_LEVELS:
            e = jnp.exp(ex[(2 + lev) * L:(3 + lev) * L])
        else:
            e = jnp.exp(_level_exponent(bc3, lev, rev))
        return (q * e).astype(BF16), (k * e).astype(BF16)

    qs, ks = q.astype(BF16), k.astype(BF16)
    att = [None] * HG_HEADS
    for lev in range(HG_LEVELS, -1, -1):
        nxt = scaled(lev - 1) if lev > 0 else None
        for h, sl in enumerate(heads):
            term = msk_ref[lev] * _dot_nt(qs[:, sl], ks[:, sl])
            att[h] = term if att[h] is None else att[h] + term
        if nxt is not None:
            qs, ks = nxt
    vb = v.astype(BF16)
    q_in = (q * jnp.exp(bcum)).astype(BF16)
    k_w = (k * jnp.exp(after)).astype(BF16)
    e_end = jnp.exp(bcum[0:1] + after[0:1])
    outs = []
    for h, sl in enumerate(heads):
        st = s_ref[h]
        outs.append(_dot(att[h].astype(BF16), vb[:, sl]) + _dot_nt(q_in[:, sl], st.astype(BF16)))
        s_ref[h] = st * e_end[:, sl] + _dot_tn(vb[:, sl], k_w[:, sl])
    return jnp.concatenate(outs, axis=1)


def _hgrn_rows_kernel(q_ref, f_ref, v_ref, lb_ref, sel_ref, msk_ref, s0_ref, y_ref, sout_ref, s_scr,
                      *, nsteps, nsub, rev):
    c = pl.program_id(1)

    @pl.when(c == 0)
    def _():
        s_scr[...] = s0_ref[0]

    lb = lb_ref[...]
    for u in (range(nsub - 1, -1, -1) if rev else range(nsub)):
        rs = slice(u * HG_CHUNK, (u + 1) * HG_CHUNK)
        y_ref[0, rs, :] = _hgrn_chunk(q_ref[0, rs, :], f_ref[0, rs, :], v_ref[0, rs, :], lb, sel_ref, msk_ref, s_scr,
                                      rev)

    @pl.when(c == nsteps - 1)
    def _():
        sout_ref[0] = s_scr[...]


def _hgrn_cols_kernel(q_ref, f_ref, v_ref, lb_ref, sel_ref, msk_ref, s0_ref, y_ref, sout_ref,
                      q_scr, f_scr, v_scr, y_scr, s_scr, *, nsteps, nsub, rev):
    c = pl.program_id(1)
    w = (nsteps - 1 - c) if rev else c
    wi = w % 8
    enter, leave = (7, 0) if rev else (0, 7)

    @pl.when(c == 0)
    def _():
        s_scr[...] = s0_ref[0]

    @pl.when(wi == enter)
    def _():
        for k in range(8):
            q_scr[k] = q_ref[0, :, 0, k, :]
            f_scr[k] = f_ref[0, :, 0, k, :]
            v_scr[k] = v_ref[0, :, 0, k, :]

    lb = lb_ref[...]
    for u in (range(nsub - 1, -1, -1) if rev else range(nsub)):
        rs = pl.ds(u * HG_CHUNK, HG_CHUNK)
        y_scr[wi, rs, :] = _hgrn_chunk(q_scr[wi, rs, :], f_scr[wi, rs, :], v_scr[wi, rs, :], lb, sel_ref, msk_ref,
                                       s_scr, rev)

    @pl.when(wi == leave)
    def _():
        for k in range(8):
            y_ref[0, :, 0, k, :] = y_scr[k]

    @pl.when(c == nsteps - 1)
    def _():
        sout_ref[0] = s_scr[...]


def _hgrn_scan(uc, u, lb, s0, *, rev, col_major):
    sel, msk = _hgrn_consts(rev)
    s_shape = s0.shape
    const_specs = [pl.BlockSpec((1, HG_F), lambda i, c: (0, 0)),
                   pl.BlockSpec((HG_SEL_ROWS, HG_CHUNK), lambda i, c: (0, 0)),
                   pl.BlockSpec((HG_LEVELS + 1, HG_CHUNK, HG_CHUNK), lambda i, c: (0, 0, 0)),
                   pl.BlockSpec((1,) + s_shape[1:], lambda i, c: (i, 0, 0, 0))]
    s_out = pl.BlockSpec((1,) + s_shape[1:], lambda i, c: (i, 0, 0, 0))
    name = ("hgrn_cols" if col_major else "hgrn_rows") + ("_bwd" if rev else "_fwd")
    cols = ((COL_HQ - UC_COLS) // HG_F, (COL_HGF - UC_COLS) // HG_F + rev, COL_HGI // HG_F)
    if col_major:
        b, r, wg, _, _ = u.shape
        nsteps = wg * 8
        nsub = r // HG_CHUNK
        st = (lambda c: nsteps - 1 - c) if rev else (lambda c: c)
        blk = (1, r, 1, 8, HG_F)
        in_specs = [pl.BlockSpec(blk, lambda i, c, cb=cb: (i, 0, st(c) // 8, 0, cb)) for cb in cols]
        out_specs = [pl.BlockSpec(blk, lambda i, c: (i, 0, st(c) // 8, 0, 0)), s_out]
        out_shape = [jax.ShapeDtypeStruct((b, r, wg, 8, HG_D), F32), jax.ShapeDtypeStruct(s_shape, F32)]
        scratch = [pltpu.VMEM((8, r, HG_F), F32)] * 4 + [pltpu.VMEM(s_shape[1:], F32)]
        body = functools.partial(_hgrn_cols_kernel, nsteps=nsteps, nsub=nsub, rev=rev)
    else:
        b, t, _ = u.shape
        nsub = min(STEP_CHUNKS, t // HG_CHUNK)
        rows = nsub * HG_CHUNK
        nsteps = t // rows
        st = (lambda c: nsteps - 1 - c) if rev else (lambda c: c)
        in_specs = [pl.BlockSpec((1, rows, HG_F), lambda i, c, cb=cb: (i, st(c), cb)) for cb in cols]
        out_specs = [pl.BlockSpec((1, rows, HG_D), lambda i, c: (i, st(c), 0)), s_out]
        out_shape = [jax.ShapeDtypeStruct((b, t, HG_D), F32), jax.ShapeDtypeStruct(s_shape, F32)]
        scratch = [pltpu.VMEM(s_shape[1:], F32)]
        body = functools.partial(_hgrn_rows_kernel, nsteps=nsteps, nsub=nsub, rev=rev)
    return pl.pallas_call(
        body,
        grid=(b, nsteps),
        in_specs=in_specs + const_specs,
        out_specs=out_specs,
        out_shape=out_shape,
        scratch_shapes=scratch,
        compiler_params=_cparams(2),
        name=name,
    )(u, u, uc, lb, sel, msk, s0)


def _pad_rows(a, rows):
    return jnp.concatenate([a, jnp.zeros((rows - a.shape[0],) + a.shape[1:], a.dtype)], axis=0)


def _lane_vec(parts, width=128):
    out = jnp.zeros((width,), F32)
    for off, vals in parts:
        out = lax.dynamic_update_slice(out, vals.astype(F32), (off,))
    return out


def _permute_w_in(w_in):
    offs = np.cumsum([0, SSD_D, SSD_XBC, 2 * SSD_HEADS, 2 * ML_D, ML_D, ML_D, 2 * ML_HEADS, 2 * ML_HEADS,
                      HG_F, 2 * HG_F, HG_D, HG_D])
    z, xbc, dt, mqk, mv, mo, ig, fg, hq, hf, hi, hg = [w_in[..., offs[i]:offs[i + 1]] for i in range(12)]
    pad = jnp.zeros(w_in.shape[:-1] + (512 - (dt.shape[-1] + ig.shape[-1] + fg.shape[-1]),), w_in.dtype)
    parts = [mqk, hi, xbc, z, hf, mv, mo, hq, hg, dt, ig, fg, pad]
    return jnp.concatenate([p.astype(BF16) for p in parts], axis=-1)


def kernel(x, c, ctx, c_ctx, ada_w, ada_b, norm_w, ffn_w_gate, ffn_w_up, ffn_w_down, w_in, w_out,
           ssd_conv_w, ssd_conv_b, ssd_a_log, ssd_dt_bias, ssd_d, ssd_norm_w, ml_conv_w, ml_conv_b,
           ml_ig_b, ml_fg_b, ml_norm_w, hg_lb_logits, hg_norm_w, final_norm_w):
    bsz, seq, d = x.shape
    ctx_len = ctx.shape[1]
    depth = ada_w.shape[0]
    rows = seq // GRID_W
    n_stream = bsz + 1

    pr = jax.nn.softmax(hg_lb_logits.astype(F32), axis=0)
    cum = jnp.cumsum(pr, axis=0)
    lb_all = cum - cum[0]

    cvec = _pad_rows(jnp.concatenate([c, c_ctx[None, :]], axis=0), 8)
    mods = _ada_mods(cvec, ada_w, ada_b)

    xl = x.reshape(bsz * seq, d)
    xc = ctx.reshape(bsz * ctx_len, d)
    tm_l = 512
    tm_c = min(512, bsz * ctx_len)
    lat_stream = lambda tm: (lambda i: i // (seq // tm))
    ctx_stream = lambda tm: (lambda i: bsz)
    fw = final_norm_w.reshape(1, d).astype(F32)
    zeros = lambda shape: jnp.zeros(shape, F32)
    wg, wu, wd = (w.astype(BF16) for w in (ffn_w_gate, ffn_w_up, ffn_w_down))
    wp = _permute_w_in(w_in)
    wo = w_out.astype(BF16)

    for l in range(depth):
        need_ctx = l < depth - 1
        mod = mods[l, :n_stream].reshape(n_stream, 3, 3, d)

        def pack(sub):
            nw = jnp.broadcast_to(norm_w[l, sub][None, None, :], (n_stream, 1, d))
            return jnp.concatenate([mod[:, sub], nw, zeros((n_stream, 4, d))], axis=1)

        pk0 = pack(0)
        xl = _ffn(xl, pk0, lat_stream(tm_l), wg, wu, wd, (l, 0), fw, tm=tm_l, tf=512)
        xc = _ffn(xc, pk0, ctx_stream(tm_c), wg, wu, wd, (l, 0), fw, tm=tm_c, tf=512)

        pk1 = pack(1)
        s_cw = _pad_rows(ssd_conv_w[l].astype(F32), 8)
        s_cb = ssd_conv_b[l].reshape(1, SSD_XBC).astype(F32)
        m_cw = _pad_rows(ml_conv_w[l].astype(F32), 8)
        m_cb = ml_conv_b[l].reshape(1, 2 * ML_D).astype(F32)
        ucl, ul = _proj(xl, pk1, lat_stream(tm_l), wp, l, m_cw, m_cb, s_cw, s_cb, tm=tm_l, tn=1536, seq=seq)
        ucc, uc = _proj(xc, pk1, ctx_stream(ctx_len), wp, l, m_cw, m_cb, s_cw, s_cb, tm=ctx_len, tn=1536,
                        seq=ctx_len)
        ucl, ul = (a.reshape(bsz, seq, a.shape[-1]) for a in (ucl, ul))
        ucc, uc = (a.reshape(bsz, ctx_len, a.shape[-1]) for a in (ucc, uc))

        a_neg = -jnp.exp(ssd_a_log[l].astype(F32))
        dv = jnp.repeat(ssd_d[l].astype(F32), SSD_HEAD_DIM).reshape(1, SSD_D)
        hp = jnp.stack([_pad_rows(jnp.stack([_lane_vec([(SM_DT + SSD_HEADS * r, ssd_dt_bias[l, r])]),
                                             _lane_vec([(SM_DT + SSD_HEADS * r, a_neg[r])])]), 8)
                        for r in (0, 1)])
        h0 = zeros((2, bsz, SSD_HEADS // 2, SSD_STATE, 2 * SSD_HEAD_DIM))
        ysc_f, ysc_b, hc = _ssd_scan(ucc, uc, hp, dv, h0)
        ysl_f, ysl_b, _ = _ssd_scan(ucl, ul, hp, dv, hc)

        gb =jnp.stack([_pad_rows(jnp.stack([_lane_vec([(SM_IG + ML_HEADS * r, ml_ig_b[l, r])]),
                                             _lane_vec([(SM_FG + ML_HEADS * r, ml_fg_b[l, r])])]), 8)
                        for r in (0, 1)])
        c0 = zeros((2, bsz, ML_HEADS, ML_HEAD_DIM, 2 * ML_HEAD_DIM))
        m0 = zeros((2, bsz, ML_HEADS, 8, 128))
        ymc_f, ymc_b, cc, mc = _mlstm_scan(ucc, uc, gb, c0, m0)
        yml_f, yml_b, _, _ = _mlstm_scan(ucl, ul, gb, cc, mc)

        lb = lb_all[l].reshape(1, HG_F)
        grid5 = lambda a: a.reshape(bsz, rows, GRID_W // 8, 8, a.shape[-1])
        yh_l, yh_c = [], []
        for rev in (0, 1):
            s0 = zeros((bsz, HG_HEADS, HG_DV, HG_DK))
            yc_, sc = _hgrn_scan(ucc, uc, lb, s0, rev=rev, col_major=False)
            yl_, _ = _hgrn_scan(grid5(ucl), grid5(ul), lb, sc, rev=rev, col_major=True)
            yh_c.append(yc_.reshape(bsz * ctx_len, HG_D))
            yh_l.append(yl_.reshape(bsz * seq, HG_D))

        nw = jnp.concatenate([ssd_norm_w[l], ml_norm_w[l], hg_norm_w[l]]).reshape(1, d).astype(F32)
        flat = lambda a: a.reshape(-1, a.shape[-1])
        xl = _merge(xl, pk1, lat_stream(256), [flat(ysl_f), flat(ysl_b)], [flat(yml_f), flat(yml_b)], yh_l,
                    flat(ul), nw, wo, l, tm=256)

        pk2 = pack(2)
        xl = _ffn(xl, pk2, lat_stream(tm_l), wg, wu, wd, (l, 1), fw, tm=tm_l, tf=512, final=not need_ctx)
        if need_ctx:
            xc = _merge(xc, pk1, ctx_stream(256), [flat(ysc_f), flat(ysc_b)], [flat(ymc_f), flat(ymc_b)], yh_c,
                        flat(uc), nw, wo, l, tm=256)
            xc = _ffn(xc, pk2, ctx_stream(tm_c), wg, wu, wd, (l, 1), fw, tm=tm_c, tf=512)

    return xl.reshape(bsz, seq, d)
```

```python
import functools

import numpy as np
import jax
import jax.numpy as jnp
from jax import lax
from jax.experimental import pallas as pl
from jax.experimental.pallas import tpu as pltpu

F32 = jnp.float32
BF16 = jnp.bfloat16

GRID_W = 64
FFN_HALF = 0.5
EPS = 1e-6
CHUNK = 128
HG_CHUNK = 128
CONV_W = 5

SSD_HEADS = 16
SSD_HEAD_DIM = 64
SSD_GROUPS = 2
SSD_STATE = 128
ML_HEADS = 4
ML_HEAD_DIM = 128
HG_HEADS = 4
HG_DK = 128
HG_DV = 128

SSD_D = SSD_HEADS * SSD_HEAD_DIM
SSD_XBC = SSD_D + 2 * SSD_GROUPS * SSD_STATE
ML_D = ML_HEADS * ML_HEAD_DIM
HG_F = HG_HEADS * HG_DK
HG_D = HG_HEADS * HG_DV

COL_MQK = 0
COL_HGI = 1024
COL_XBC = 1536
COL_Z = 3072
COL_HGF = 4096
COL_MV = 5120
COL_MO = 5632
COL_HQ = 6144
COL_HGG = 6656
COL_SM = 7168
U_COLS = 7680
UC_COLS = 3072
SM_DT = 0
SM_IG = 2 * SSD_HEADS
SM_FG = 2 * SSD_HEADS + 2 * ML_HEADS

NEG = -1e30

FFN_TM, FFN_TF = 512, 512
PROJ_TM, PROJ_TN = 512, 1536
MERGE_TM = 256
VMEM_LIMIT_BYTES = 56 * 1024 * 1024


def _cparams(n_axes):
    return pltpu.CompilerParams(dimension_semantics=("arbitrary",) * n_axes,
                                vmem_limit_bytes=VMEM_LIMIT_BYTES)


def _sigmoid(x):
    return 1.0 / (1.0 + jnp.exp(-x))


def _silu(x):
    return x * _sigmoid(x)


def _log1p_exp_neg(y):
    return jnp.log(1.0 + jnp.exp(-y))


def _log_sigmoid(x):
    return jnp.minimum(x, 0.0) - _log1p_exp_neg(jnp.abs(x))


def _softplus(x):
    return jnp.maximum(x, 0.0) + jnp.log1p(jnp.exp(-jnp.abs(x)))


def _dot(a, b):
    return jnp.dot(a, b, preferred_element_type=F32)


def _dot_nt(a, b):
    return lax.dot_general(a, b, (((1,), (1,)), ((), ())), preferred_element_type=F32)


def _dot_tn(a, b):
    return lax.dot_general(a, b, (((0,), (0,)), ((), ())), preferred_element_type=F32)


def _dot_sel(sel_bf16, x):
    hi = x.astype(BF16)
    r1 = x - hi.astype(F32)
    mid = r1.astype(BF16)
    lo = (r1 - mid.astype(F32)).astype(BF16)
    return _dot(sel_bf16, hi) + _dot(sel_bf16, mid) + _dot(sel_bf16, lo)


def _norm_mod(x, pk):
    ms = jnp.mean(x * x, axis=-1, keepdims=True)
    xn = (x * lax.rsqrt(ms + EPS)) * pk[3:4]
    return xn * (1.0 + pk[1:2]) + pk[0:1]


def _ada_kernel(c_ref, w_ref, b_ref, o_ref):
    c = _silu(c_ref[...]).astype(BF16)
    o_ref[0] = _dot(c, w_ref[0].astype(BF16)) + b_ref[0]


def _ada_mods(cvec, ada_w, ada_b):
    depth, d, n = ada_w.shape
    tn = 1024
    return pl.pallas_call(
        _ada_kernel,
        grid=(depth, n // tn),
        in_specs=[pl.BlockSpec((8, d), lambda l, j: (0, 0)),
                  pl.BlockSpec((1, d, tn), lambda l, j: (l, 0, j)),
                  pl.BlockSpec((1, 1, tn), lambda l, j: (l, 0, j))],
        out_specs=pl.BlockSpec((1, 8, tn), lambda l, j: (l, 0, j)),
        out_shape=jax.ShapeDtypeStruct((depth, 8, n), F32),
        compiler_params=_cparams(2),
        name="ada_mods",
    )(cvec, ada_w, ada_b.reshape(depth, 1, n))


def _ffn_kernel(x_ref, pk_ref, wg_ref, wu_ref, wd_ref, fw_ref, o_ref, h_scr, acc_scr, *, nf, final):
    f = pl.program_id(1)

    @pl.when(f == 0)
    def _():
        h_scr[...] = _norm_mod(x_ref[...], pk_ref[0]).astype(BF16)
        acc_scr[...] = jnp.zeros_like(acc_scr)

    h = h_scr[...]
    g = _dot(h, wg_ref[...])
    u = _dot(h, wu_ref[...])
    a = (_silu(g) * u).astype(BF16)
    acc_scr[...] += _dot(a, wd_ref[...])

    @pl.when(f == nf - 1)
    def _():
        y = x_ref[...] + (FFN_HALF * pk_ref[0][2:3]) * acc_scr[...]
        if final:
            ms = jnp.mean(y * y, axis=-1, keepdims=True)
            y = (y * lax.rsqrt(ms + EPS)) * fw_ref[...]
        o_ref[...] = y


def _ffn(x, pack, stream_of_tile, wg, wu, wd, ls, fw, *, tm, tf, final=False):
    m, d = x.shape
    l, s = ls
    nf = wg.shape[-1] // tf
    return pl.pallas_call(
        functools.partial(_ffn_kernel, nf=nf, final=final),
        grid=(m // tm, nf),
        in_specs=[pl.BlockSpec((tm, d), lambda i, f: (i, 0)),
                  pl.BlockSpec((1, 8, d), lambda i, f: (stream_of_tile(i), 0, 0)),
                  pl.BlockSpec((None, None, d, tf), lambda i, f: (l, s, 0, f)),
                  pl.BlockSpec((None, None, d, tf), lambda i, f: (l, s, 0, f)),
                  pl.BlockSpec((None, None, tf, d), lambda i, f: (l, s, f, 0)),
                  pl.BlockSpec((1, d), lambda i, f: (0, 0))],
        out_specs=pl.BlockSpec((tm, d), lambda i, f: (i, 0)),
        out_shape=jax.ShapeDtypeStruct((m, d), F32),
        scratch_shapes=[pltpu.VMEM((tm, d), BF16), pltpu.VMEM((tm, d), F32)],
        compiler_params=_cparams(2),
        name="ffn",
    )(x, pack, wg, wu, wd, fw)


HALO = 16


def _proj_kernel(x_ref, xp_ref, xn_ref, pk_ref, w_ref, cwa_ref, cba_ref, cwb_ref, cbb_ref, oc_ref, o_ref,
                 h_scr, raw_scr, *, tm, tiles_per_seq):
    i = pl.program_id(0)
    j = pl.program_id(1)
    tn = o_ref.shape[1]

    def conv_tile(slot, cw_ref, cb_ref, ncols):
        keep_p = (i % tiles_per_seq != 0).astype(F32)
        keep_n = (i % tiles_per_seq != tiles_per_seq - 1).astype(F32)
        rb, cb = 64, 256
        for c0 in range(0, ncols, cb):
            w8 = [jnp.broadcast_to(cw_ref[t:t + 1, c0:c0 + cb], (8, cb))[None] for t in range(CONV_W)]
            b8 = jnp.broadcast_to(cb_ref[:, c0:c0 + cb], (8, cb))[None]
            for r0 in range(0, tm, rb):
                blk = raw_scr[slot, HALO + r0 - 8:HALO + r0 + rb + 8, c0:c0 + cb]
                if r0 == 0:
                    blk = jnp.concatenate([blk[:8] * keep_p, blk[8:]], axis=0)
                if r0 == tm - rb:
                    blk = jnp.concatenate([blk[:rb + 8], blk[rb + 8:] * keep_n], axis=0)
                acc = b8
                for t in range(CONV_W):
                    sh = blk if t == CONV_W // 2 else pltpu.roll(blk, (CONV_W // 2 - t) % (rb + 16), axis=0)
                    acc = acc + _tiles(sh[8:8 + rb]) * w8[t]
                oc_ref[r0:r0 + rb, c0:c0 + cb] = _silu(acc).reshape(rb, cb)
        if ncols < tn:
            oc_ref[:, ncols:] = raw_scr[slot, HALO:HALO + tm, ncols:]

    @pl.when(j == 0)
    def _():
        pk = pk_ref[0]
        h_scr[0:HALO, :] = _norm_mod(xp_ref[...], pk).astype(BF16)
        h_scr[HALO:HALO + tm, :] = _norm_mod(x_ref[...], pk).astype(BF16)
        h_scr[HALO + tm:, :] = _norm_mod(xn_ref[...], pk).astype(BF16)
        raw_scr[0] = _dot(h_scr[...], w_ref[...])

    @pl.when(j == 1)
    def _():
        conv_tile(0, cwa_ref, cba_ref, 2 * ML_D)
        raw_scr[1] = _dot(h_scr[...], w_ref[...])

    @pl.when(j == 2)
    def _():
        conv_tile(1, cwb_ref, cbb_ref, SSD_XBC)
        o_ref[...] = _dot(h_scr[HALO:HALO + tm, :], w_ref[...])

    @pl.when(j > 2)
    def _():
        o_ref[...] = _dot(h_scr[HALO:HALO + tm, :], w_ref[...])


def _proj(x, pack, stream_of_tile, w, layer, cwa, cba, cwb, cbb, *, tm, tn, seq):
    m, d = x.shape
    assert (COL_MQK, COL_XBC, UC_COLS) == (0, tn, 2 * tn)
    per = tm // HALO
    last = m // HALO - 1
    nconv = UC_COLS // tn
    return pl.pallas_call(
        functools.partial(_proj_kernel, tm=tm, tiles_per_seq=seq // tm),
        grid=(m // tm, U_COLS // tn),
        in_specs=[pl.BlockSpec((tm, d), lambda i, j: (i, 0)),
                  pl.BlockSpec((HALO, d), lambda i, j: (jnp.maximum(i * per - 1, 0), 0)),
                  pl.BlockSpec((HALO, d), lambda i, j: (jnp.minimum((i + 1) * per, last), 0)),
                  pl.BlockSpec((1, 8, d), lambda i, j: (stream_of_tile(i), 0, 0)),
                  pl.BlockSpec((None, d, tn), lambda i, j: (layer, 0, j)),
                  pl.BlockSpec((8, 2 * ML_D), lambda i, j: (0, 0)),
                  pl.BlockSpec((1, 2 * ML_D), lambda i, j: (0, 0)),
                  pl.BlockSpec((8, SSD_XBC), lambda i, j: (0, 0)),
                  pl.BlockSpec((1, SSD_XBC), lambda i, j: (0, 0))],
        out_specs=[pl.BlockSpec((tm, tn), lambda i, j: (i, jnp.clip(j - 1, 0, nconv - 1))),
                   pl.BlockSpec((tm, tn), lambda i, j: (i, jnp.maximum(j - nconv, 0)))],
        out_shape=[jax.ShapeDtypeStruct((m, UC_COLS), F32), jax.ShapeDtypeStruct((m, U_COLS - UC_COLS), F32)],
        scratch_shapes=[pltpu.VMEM((tm + 2 * HALO, d), BF16), pltpu.VMEM((2, tm + 2 * HALO, tn), F32)],
        compiler_params=_cparams(2),
        name="in_proj",
    )(x, x, x, pack, w, cwa, cba, cwb, cbb)


def _group_norm(y, w, group):
    outs = []
    for g in range(y.shape[1] // group):
        yg = y[:, g * group:(g + 1) * group]
        ms = jnp.mean(yg * yg, axis=-1, keepdims=True)
        outs.append(yg * lax.rsqrt(ms + EPS))
    return jnp.concatenate(outs, axis=1) * w


def _merge_kernel(x_ref, pk_ref, sf_ref, sb_ref, mf_ref, mb_ref, hf_ref, hb_ref, z_ref, mo_ref, hg_ref,
                  nw_ref, wo_ref, o_ref):
    nw = nw_ref[...]
    ys = (sf_ref[...] + sb_ref[...]) * _silu(z_ref[...])
    ys = _group_norm(ys, nw[:, :SSD_D], SSD_D // SSD_GROUPS)
    ym = _group_norm(mf_ref[...] + mb_ref[...], nw[:, SSD_D:SSD_D + ML_D], ML_HEAD_DIM) * _sigmoid(mo_ref[...])
    yh = _group_norm(hf_ref[...] + hb_ref[...], nw[:, SSD_D + ML_D:], HG_DV) * _silu(hg_ref[...])
    y = jnp.concatenate([ys, ym, yh], axis=1).astype(BF16)
    o_ref[...] = x_ref[...] + pk_ref[0][2:3] * _dot(y, wo_ref[...])


def _merge(x, pack, stream_of_tile, ys, ym, yh, u, nw, wo, layer, *, tm):
    m, d = x.shape

    def rows(width, colblk):
        return pl.BlockSpec((tm, width), lambda i: (i, colblk))

    return pl.pallas_call(
        _merge_kernel,
        grid=(m // tm,),
        in_specs=[rows(d, 0),
                  pl.BlockSpec((1, 8, d), lambda i: (stream_of_tile(i), 0, 0)),
                  rows(SSD_D, 0), rows(SSD_D, 0), rows(ML_D, 0), rows(ML_D, 0), rows(HG_D, 0), rows(HG_D, 0),
                  rows(SSD_D, (COL_Z - UC_COLS) // SSD_D), rows(ML_D, (COL_MO - UC_COLS) // ML_D),
                  rows(HG_D, (COL_HGG - UC_COLS) // HG_D),
                  pl.BlockSpec((1, d), lambda i: (0, 0)),
                  pl.BlockSpec((None, d, d), lambda i: (layer, 0, 0))],
        out_specs=rows(d, 0),
        out_shape=jax.ShapeDtypeStruct((m, d), F32),
        compiler_params=_cparams(1),
        name="merge_out",
    )(x, pack, ys[0], ys[1], ym[0], ym[1], yh[0], yh[1], u, u, u, nw, wo)


STEP_CHUNKS = 4


def _scan_tri(length, rev):
    r = lax.broadcasted_iota(jnp.int32, (length, length), 0)
    c = lax.broadcasted_iota(jnp.int32, (length, length), 1)
    return (c >= r) if rev else (c <= r)


def _scan_tri3(length, rev):
    hi = lax.broadcasted_iota(jnp.int32, (length // 8, 8, length), 0)
    lo = lax.broadcasted_iota(jnp.int32, (length // 8, 8, length), 1)
    c = lax.broadcasted_iota(jnp.int32, (length // 8, 8, length), 2)
    r = hi * 8 + lo
    return (c >= r) if rev else (c <= r)


def _tiles(x):
    return x.reshape(x.shape[0] // 8, 8, x.shape[1])


def _row8(x, i):
    return jnp.broadcast_to(x[i:i + 1, :], (8, x.shape[1]))[None]


def _ssd_prelude(xbc, raw, hp, rev):
    L = CHUNK
    bm = xbc[:, SSD_D:SSD_D + SSD_GROUPS * SSD_STATE]
    cm = xbc[:, SSD_D + SSD_GROUPS * SSD_STATE:]
    dt = _softplus(raw + hp[0:1, :])
    la = dt * hp[1:2, :]
    a = _dot_sel(_scan_tri(L, rev).astype(BF16), la)
    a_end = a[0:1, :] if rev else a[L - 1:L, :]
    groups = []
    for g in range(SSD_GROUPS):
        cg = cm[:, g * SSD_STATE:(g + 1) * SSD_STATE]
        bg = bm[:, g * SSD_STATE:(g + 1) * SSD_STATE]
        groups.append((_tiles(_dot_nt(cg.astype(BF16), bg.astype(BF16))), _tiles(cg), _tiles(bg.T)))
    return a, a.T, dt.T, (jnp.exp(a_end - a) * dt).T, jnp.exp(a_end), groups


def _ssd_pair(xp, pre, pair, hpair, dv, rev):
    L = CHUNK
    P = SSD_HEAD_DIM
    a, a_t, dt_t, wf_t, eend, groups = pre
    cb, cg3, bg_t = groups[pair // (SSD_HEADS // SSD_GROUPS // 2)]
    off = SM_DT + SSD_HEADS * rev
    mask = _scan_tri3(L, rev)
    first = lax.broadcasted_iota(jnp.int32, (1, 2 * P), 1) < P
    lhs, rhs, blhs, xms, ends = [], [], [], [], []
    for k in range(2):
        ln = off + 2 * pair + k
        sel = first if k == 0 else jnp.logical_not(first)
        acol = _tiles(jnp.broadcast_to(a[:, ln:ln + 1], (L, L)))
        dec = jnp.exp(jnp.where(mask, acol - _row8(a_t, ln), NEG))
        lhs += [(cb * dec * _row8(dt_t, ln)).reshape(L, L), (cg3 * jnp.exp(acol)).reshape(L, SSD_STATE)]
        xm = jnp.where(sel, xp, 0.0)
        rhs += [xm, jnp.where(sel, hpair, 0.0)]
        blhs.append((bg_t * _row8(wf_t, ln)).reshape(SSD_STATE, L))
        xms.append(xm)
        ends.append(eend[:, ln:ln + 1])
    xcat = jnp.concatenate(xms, axis=0).astype(BF16)
    y = _dot(jnp.concatenate(lhs, axis=1).astype(BF16), jnp.concatenate(rhs, axis=0).astype(BF16))
    if not rev:
        y = y + dv * xp
    upd = _dot(jnp.concatenate(blhs, axis=1).astype(BF16), xcat)
    return y, hpair * jnp.where(first, ends[0], ends[1]) + upd


def _ssd_kernel(xf_ref, sf_ref, xb_ref, sb_ref, hp_ref, dv_ref, h0_ref, yf_ref, yb_ref, hout_ref, h_scr,
                *, nsteps, nsub):
    c = pl.program_id(1)
    W = 2 * SSD_HEAD_DIM

    @pl.when(c == 0)
    def _():
        h_scr[...] = h0_ref[:, 0]

    npair = SSD_HEADS // 2
    states = [[h_scr[d, p] for p in range(npair)] for d in range(2)]
    streams = ((xf_ref, sf_ref, yf_ref), (xb_ref, sb_ref, yb_ref))
    for k in range(nsub):
        ctx = []
        for d, (x_ref, s_ref, y_ref) in enumerate(streams):
            u = nsub - 1 - k if d else k
            rs = slice(u * CHUNK, (u + 1) * CHUNK)
            ctx.append((rs, _ssd_prelude(x_ref[0, rs, :], s_ref[0, rs, 0:128], hp_ref[d], d)))
        for p in range(npair):
            for d, (x_ref, s_ref, y_ref) in enumerate(streams):
                rs, pre = ctx[d]
                y, states[d][p] = _ssd_pair(x_ref[0, rs, p * W:(p + 1) * W], pre, p, states[d][p],
                                            dv_ref[:, p * W:(p + 1) * W], d)
                y_ref[0, rs, p * W:(p + 1) * W] = y
    for d in range(2):
        for p in range(npair):
            h_scr[d, p] = states[d][p]

    @pl.when(c == nsteps - 1)
    def _():
        hout_ref[:, 0] = h_scr[...]


def _ssd_scan(uc3, up3, hp, dv, h0):
    b, t, _ = uc3.shape
    nsub = min(STEP_CHUNKS, t // CHUNK)
    rows = nsub * CHUNK
    nsteps = t // rows
    fwd = lambda c: c
    bwd = lambda c: nsteps - 1 - c
    st_shape = h0.shape

    def stream_specs(step_of):
        return [pl.BlockSpec((1, rows, SSD_XBC), lambda i, c: (i, step_of(c), COL_XBC // SSD_XBC)),
                pl.BlockSpec((1, rows, 512), lambda i, c: (i, step_of(c), (COL_SM - UC_COLS) // 512))]

    st_spec = pl.BlockSpec((2, 1) + st_shape[2:], lambda i, c: (0, i, 0, 0, 0))
    return pl.pallas_call(
        functools.partial(_ssd_kernel, nsteps=nsteps, nsub=nsub),
        grid=(b, nsteps),
        in_specs=stream_specs(fwd) + stream_specs(bwd) + [
            pl.BlockSpec((2, 8, 128), lambda i, c: (0, 0, 0)),
            pl.BlockSpec((1, SSD_D), lambda i, c: (0, 0)),
            st_spec],
        out_specs=[pl.BlockSpec((1, rows, SSD_D), lambda i, c: (i, fwd(c), 0)),
                   pl.BlockSpec((1, rows, SSD_D), lambda i, c: (i, bwd(c), 0)),
                   st_spec],
        out_shape=[jax.ShapeDtypeStruct((b, t, SSD_D), F32), jax.ShapeDtypeStruct((b, t, SSD_D), F32),
                   jax.ShapeDtypeStruct(st_shape, F32)],
        scratch_shapes=[pltpu.VMEM((2,) + st_shape[2:], F32)],
        compiler_params=_cparams(2),
        name="ssd_scan",
    )(uc3, up3, uc3, up3, hp, dv, h0)


def _mlstm_prelude(raw, gb, rev):
    logi = raw + gb[0:1, :]
    logf = _log_sigmoid(raw + gb[1:2, :])
    bcum = _dot_sel(_scan_tri(CHUNK, rev).astype(BF16), logf)
    return bcum, bcum.T, logi.T


def _mlstm_head(qh, kh, vh, pre, h, state, rev):
    L = CHUNK
    K = ML_HEAD_DIM
    bcum, b_t, logi_t = pre
    li = SM_IG + ML_HEADS * rev + h
    lf = SM_FG + ML_HEADS * rev + h
    end = 0 if rev else L - 1
    mask = _scan_tri3(L, rev)
    kh = kh * K ** -0.5
    vaug = jnp.concatenate([vh.astype(BF16), jnp.ones((L, K), BF16)], axis=1)
    c_old, m_old = state
    brow = b_t[lf:lf + 1, :]
    irow = logi_t[li:li + 1, :]
    bcol = _tiles(jnp.broadcast_to(bcum[:, lf:lf + 1], (L, L)))
    dm = jnp.where(mask, bcol - jnp.broadcast_to(brow - irow, (8, L))[None], NEG)
    gfull = bcol + m_old
    mt = jnp.maximum(gfull, jnp.max(dm, axis=2, keepdims=True))
    qkm = _tiles(_dot_nt(qh.astype(BF16), kh.astype(BF16))) * jnp.exp(dm - mt)
    qin = _tiles(qh) * jnp.exp(gfull - mt)
    lhs = jnp.concatenate([qkm.reshape(L, L), qin.reshape(L, K)], axis=1).astype(BF16)
    rhs = jnp.concatenate([vaug, c_old.astype(BF16)], axis=0)
    s = _dot(lhs, rhs)
    den = jnp.maximum(jnp.abs(s[:, K:]), jnp.exp(-mt).reshape(L, L))
    bl = brow[:, end:end + 1]
    ds = bl - brow + irow
    m_new = jnp.maximum(bl + m_old, jnp.max(ds, axis=1, keepdims=True))
    ws = jnp.exp(ds - m_new)
    keep = jnp.exp(bl + m_old - m_new)
    c_new = c_old * keep + _dot((kh.T * ws).astype(BF16), vaug)
    return s[:, :K] / den, (c_new, m_new)


def _mlstm_kernel(qf_ref, vf_ref, sf_ref, qb_ref, vb_ref, sb_ref, gb_ref, c0_ref, m0_ref,
                  yf_ref, yb_ref, cout_ref, mout_ref, c_scr, m_scr, *, nsteps, nsub):
    c = pl.program_id(1)
    K = ML_HEAD_DIM

    @pl.when(c == 0)
    def _():
        c_scr[...] = c0_ref[:, 0]
        m_scr[...] = m0_ref[:, 0]

    states = [[(c_scr[d, h], m_scr[d, h][0:1, 0:1]) for h in range(ML_HEADS)] for d in range(2)]
    streams = ((qf_ref, vf_ref, sf_ref, yf_ref), (qb_ref, vb_ref, sb_ref, yb_ref))
    for k in range(nsub):
        ctx = []
        for d, (q_ref, v_ref, s_ref, y_ref) in enumerate(streams):
            u = nsub - 1 - k if d else k
            rs = slice(u * CHUNK, (u + 1) * CHUNK)
            ctx.append((rs, _mlstm_prelude(s_ref[0, rs, 0:128], gb_ref[d], d)))
        for h in range(ML_HEADS):
            for d, (q_ref, v_ref, s_ref, y_ref) in enumerate(streams):
                rs, pre = ctx[d]
                y, states[d][h] = _mlstm_head(q_ref[0, rs, h * K:(h + 1) * K],
                                              q_ref[0, rs, ML_D + h * K:ML_D + (h + 1) * K],
                                              v_ref[0, rs, h * K:(h + 1) * K], pre, h, states[d][h], d)
                y_ref[0, rs, h * K:(h + 1) * K] = y
    for d in range(2):
        for h in range(ML_HEADS):
            c_scr[d, h] = states[d][h][0]
            m_scr[d, h] = jnp.broadcast_to(states[d][h][1], (8, 128))

    @pl.when(c == nsteps - 1)
    def _():
        cout_ref[:, 0] = c_scr[...]
        mout_ref[:, 0] = m_scr[...]


def _mlstm_scan(uc3, up3, gb, c0, m0):
    b, t, _ = uc3.shape
    nsub = min(STEP_CHUNKS, t // CHUNK)
    rows = nsub * CHUNK
    nsteps = t // rows
    fwd = lambda c: c
    bwd = lambda c: nsteps - 1 - c

    def stream_specs(step_of):
        return [pl.BlockSpec((1, rows, 2 * ML_D), lambda i, c: (i, step_of(c), COL_MQK // (2 * ML_D))),
                pl.BlockSpec((1, rows, ML_D), lambda i, c: (i, step_of(c), (COL_MV - UC_COLS) // ML_D)),
                pl.BlockSpec((1, rows, 512), lambda i, c: (i, step_of(c), (COL_SM - UC_COLS) // 512))]

    c_spec = pl.BlockSpec((2, 1) + c0.shape[2:], lambda i, c: (0, i, 0, 0, 0))
    m_spec = pl.BlockSpec((2, 1) + m0.shape[2:], lambda i, c: (0, i, 0, 0, 0))
    return pl.pallas_call(
        functools.partial(_mlstm_kernel, nsteps=nsteps, nsub=nsub),
        grid=(b, nsteps),
        in_specs=stream_specs(fwd) + stream_specs(bwd) + [
            pl.BlockSpec((2, 8, 128), lambda i, c: (0, 0, 0)),
            c_spec, m_spec],
        out_specs=[pl.BlockSpec((1, rows, ML_D), lambda i, c: (i, fwd(c), 0)),
                   pl.BlockSpec((1, rows, ML_D), lambda i, c: (i, bwd(c), 0)),
                   c_spec, m_spec],
        out_shape=[jax.ShapeDtypeStruct((b, t, ML_D), F32), jax.ShapeDtypeStruct((b, t, ML_D), F32),
                   jax.ShapeDtypeStruct(c0.shape, F32), jax.ShapeDtypeStruct(m0.shape, F32)],
        scratch_shapes=[pltpu.VMEM((2,) + c0.shape[2:], F32), pltpu.VMEM((2,) + m0.shape[2:], F32)],
        compiler_params=_cparams(2),
        name="mlstm_scan",
    )(uc3, up3, up3, uc3, up3, up3, gb, c0, m0)


HG_LEVELS = 7
HG_SEL_LEVELS = 3
HG_SEL_ROWS = (2 + HG_SEL_LEVELS) * HG_CHUNK


def _hgrn_consts(rev):
    L = HG_CHUNK
    pos = np.arange(L)
    p = pos[::-1] if rev else pos
    cum = (p[None, :] <= p[:, None])
    after = (p[None, :] > p[:, None])
    sels = [cum, after]
    masks = []
    for lev in range(HG_LEVELS):
        m = 1 << lev
        blk = p // (2 * m)
        late = (p % (2 * m)) >= m
        ref = blk * 2 * m + m - 1
        pj = p[None, :]
        sel = np.where(late[:, None], (pj > ref[:, None]) & (pj <= p[:, None]),
                       (pj > p[:, None]) & (pj <= ref[:, None]))
        if lev < HG_SEL_LEVELS:
            sels.append(sel)
        masks.append((blk[:, None] == blk[None, :]) & late[:, None] & (~late)[None, :])
    masks.append(np.eye(L, dtype=bool))
    sel_all = np.concatenate(sels, axis=0).astype(np.float32)
    mask_all = np.stack(masks, axis=0).astype(np.float32)
    return jnp.asarray(sel_all, BF16), jnp.asarray(mask_all, F32)


def _level_exponent(bc3, lev, rev):
    m8 = (1 << lev) // 8
    refs = {}
    tiles = []
    for tau in range(bc3.shape[0]):
        blk, w = divmod(tau, 2 * m8)
        upper = w >= m8
        if blk not in refs:
            row = bc3[blk * 2 * m8 + m8, 0:1, :] if rev else bc3[blk * 2 * m8 + m8 - 1, 7:8, :]
            refs[blk] = jnp.broadcast_to(row, bc3.shape[1:])
        late = (not upper) if rev else upper
        tiles.append(bc3[tau] - refs[blk] if late else refs[blk] - bc3[tau])
    return jnp.stack(tiles, axis=0).reshape(bc3.shape[0] * 8, bc3.shape[2])


def _hgrn_chunk(q, fr, v, lb, sel_ref, msk_ref, s_ref, rev):
    L = HG_CHUNK
    K = HG_DK
    x1 = jnp.log(lb)
    x2 = jnp.log1p(-lb) + _log_sigmoid(fr)
    logf = jnp.maximum(x1, x2) + _log1p_exp_neg(jnp.abs(x1 - x2))
    k = (1.0 - lb) * _sigmoid(-fr)
    ex = _dot_sel(sel_ref[...], logf)
    bcum = ex[0:L]
    after = ex[L:2 * L]
    heads = [slice(h * K, (h + 1) * K) for h in range(HG_HEADS)]

    bc3 = _tiles(bcum)

    def scaled(lev):
        if lev < HG_SEL_LEVELS:
            e = jnp.exp(ex[(2 + lev) * L:(3 + lev) * L])
        else:
            e = jnp.exp(_level_exponent(bc3, lev, rev))
        return (q * e).astype(BF16), (k * e).astype(BF16)

    qs, ks = q.astype(BF16), k.astype(BF16)
    att = [None] * HG_HEADS
    for lev in range(HG_LEVELS, -1, -1):
        nxt = scaled(lev - 1) if lev > 0 else None
        for h, sl in enumerate(heads):
            term = msk_ref[lev] * _dot_nt(qs[:, sl], ks[:, sl])
            att[h] = term if att[h] is None else att[h] + term
        if nxt is not None:
            qs, ks = nxt
    vb = v.astype(BF16)
    q_in = (q * jnp.exp(bcum)).astype(BF16)
    k_w = (k * jnp.exp(after)).astype(BF16)
    e_end = jnp.exp(bcum[0:1] + after[0:1])
    outs = []
    for h, sl in enumerate(heads):
        st = s_ref[h]
        outs.append(_dot(att[h].astype(BF16), vb[:, sl]) + _dot_nt(q_in[:, sl], st.astype(BF16)))
        s_ref[h] = st * e_end[:, sl] + _dot_tn(vb[:, sl], k_w[:, sl])
    return jnp.concatenate(outs, axis=1)


def _hgrn_rows_kernel(q_ref, f_ref, v_ref, lb_ref, sel_ref, msk_ref, s0_ref, y_ref, sout_ref, s_scr,
                      *, nsteps, nsub, rev):
    c = pl.program_id(1)

    @pl.when(c == 0)
    def _():
        s_scr[...] = s0_ref[0]

    lb = lb_ref[...]
    for u in (range(nsub - 1, -1, -1) if rev else range(nsub)):
        rs = slice(u * HG_CHUNK, (u + 1) * HG_CHUNK)
        y_ref[0, rs, :] = _hgrn_chunk(q_ref[0, rs, :], f_ref[0, rs, :], v_ref[0, rs, :], lb, sel_ref, msk_ref, s_scr,
                                      rev)

    @pl.when(c == nsteps - 1)
    def _():
        sout_ref[0] = s_scr[...]


def _hgrn_cols_kernel(q_ref, f_ref, v_ref, lb_ref, sel_ref, msk_ref, s0_ref, y_ref, sout_ref,
                      q_scr, f_scr, v_scr, y_scr, s_scr, *, nsteps, nsub, rev):
    c = pl.program_id(1)
    w = (nsteps - 1 - c) if rev else c
    wi = w % 8
    enter, leave = (7, 0) if rev else (0, 7)

    @pl.when(c == 0)
    def _():
        s_scr[...] = s0_ref[0]

    @pl.when(wi == enter)
    def _():
        for k in range(8):
            q_scr[k] = q_ref[0, :, 0, k, :]
            f_scr[k] = f_ref[0, :, 0, k, :]
            v_scr[k] = v_ref[0, :, 0, k, :]

    lb = lb_ref[...]
    for u in (range(nsub - 1, -1, -1) if rev else range(nsub)):
        rs = pl.ds(u * HG_CHUNK, HG_CHUNK)
        y_scr[wi, rs, :] = _hgrn_chunk(q_scr[wi, rs, :], f_scr[wi, rs, :], v_scr[wi, rs, :], lb, sel_ref, msk_ref,
                                       s_scr, rev)

    @pl.when(wi == leave)
    def _():
        for k in range(8):
            y_ref[0, :, 0, k, :] = y_scr[k]

    @pl.when(c == nsteps - 1)
    def _():
        sout_ref[0] = s_scr[...]


def _hgrn_scan(uc, u, lb, s0, *, rev, col_major):
    sel, msk = _hgrn_consts(rev)
    s_shape = s0.shape
    const_specs = [pl.BlockSpec((1, HG_F), lambda i, c: (0, 0)),
                   pl.BlockSpec((HG_SEL_ROWS, HG_CHUNK), lambda i, c: (0, 0)),
                   pl.BlockSpec((HG_LEVELS + 1, HG_CHUNK, HG_CHUNK), lambda i, c: (0, 0, 0)),
                   pl.BlockSpec((1,) + s_shape[1:], lambda i, c: (i, 0, 0, 0))]
    s_out = pl.BlockSpec((1,) + s_shape[1:], lambda i, c: (i, 0, 0, 0))
    name = ("hgrn_cols" if col_major else "hgrn_rows") + ("_bwd" if rev else "_fwd")
    cols = ((COL_HQ - UC_COLS) // HG_F, (COL_HGF - UC_COLS) // HG_F + rev, COL_HGI // HG_F)
    if col_major:
        b, r, wg, _, _ = u.shape
        nsteps = wg * 8
        nsub = r // HG_CHUNK
        st = (lambda c: nsteps - 1 - c) if rev else (lambda c: c)
        blk = (1, r, 1, 8, HG_F)
        in_specs = [pl.BlockSpec(blk, lambda i, c, cb=cb: (i, 0, st(c) // 8, 0, cb)) for cb in cols]
        out_specs = [pl.BlockSpec(blk, lambda i, c: (i, 0, st(c) // 8, 0, 0)), s_out]
        out_shape = [jax.ShapeDtypeStruct((b, r, wg, 8, HG_D), F32), jax.ShapeDtypeStruct(s_shape, F32)]
        scratch = [pltpu.VMEM((8, r, HG_F), F32)] * 4 + [pltpu.VMEM(s_shape[1:], F32)]
        body = functools.partial(_hgrn_cols_kernel, nsteps=nsteps, nsub=nsub, rev=rev)
    else:
        b, t, _ = u.shape
        nsub = min(STEP_CHUNKS, t // HG_CHUNK)
        rows = nsub * HG_CHUNK
        nsteps = t // rows
        st = (lambda c: nsteps - 1 - c) if rev else (lambda c: c)
        in_specs = [pl.BlockSpec((1, rows, HG_F), lambda i, c, cb=cb: (i, st(c), cb)) for cb in cols]
        out_specs = [pl.BlockSpec((1, rows, HG_D), lambda i, c: (i, st(c), 0)), s_out]
        out_shape = [jax.ShapeDtypeStruct((b, t, HG_D), F32), jax.ShapeDtypeStruct(s_shape, F32)]
        scratch = [pltpu.VMEM(s_shape[1:], F32)]
        body = functools.partial(_hgrn_rows_kernel, nsteps=nsteps, nsub=nsub, rev=rev)
    return pl.pallas_call(
        body,
        grid=(b, nsteps),
        in_specs=in_specs + const_specs,
        out_specs=out_specs,
        out_shape=out_shape,
        scratch_shapes=scratch,
        compiler_params=_cparams(2),
        name=name,
    )(u, u, uc, lb, sel, msk, s0)


def _pad_rows(a, rows):
    return jnp.concatenate([a, jnp.zeros((rows - a.shape[0],) + a.shape[1:], a.dtype)], axis=0)


def _lane_vec(parts, width=128):
    out = jnp.zeros((width,), F32)
    for off, vals in parts:
        out = lax.dynamic_update_slice(out, vals.astype(F32), (off,))
    return out


def _permute_w_in(w_in):
    offs = np.cumsum([0, SSD_D, SSD_XBC, 2 * SSD_HEADS, 2 * ML_D, ML_D, ML_D, 2 * ML_HEADS, 2 * ML_HEADS,
                      HG_F, 2 * HG_F, HG_D, HG_D])
    z, xbc, dt, mqk, mv, mo, ig, fg, hq, hf, hi, hg = [w_in[..., offs[i]:offs[i + 1]] for i in range(12)]
    pad = jnp.zeros(w_in.shape[:-1] + (512 - (dt.shape[-1] + ig.shape[-1] + fg.shape[-1]),), w_in.dtype)
    parts = [mqk, hi, xbc, z, hf, mv, mo, hq, hg, dt, ig, fg, pad]
    return jnp.concatenate([p.astype(BF16) for p in parts], axis=-1)


def kernel(x, c, ctx, c_ctx, ada_w, ada_b, norm_w, ffn_w_gate, ffn_w_up, ffn_w_down, w_in, w_out,
           ssd_conv_w, ssd_conv_b, ssd_a_log, ssd_dt_bias, ssd_d, ssd_norm_w, ml_conv_w, ml_conv_b,
           ml_ig_b, ml_fg_b, ml_norm_w, hg_lb_logits, hg_norm_w, final_norm_w):
    bsz, seq, d = x.shape
    ctx_len = ctx.shape[1]
    depth = ada_w.shape[0]
    rows = seq // GRID_W
    n_stream = bsz + 1

    pr = jax.nn.softmax(hg_lb_logits.astype(F32), axis=0)
    cum = jnp.cumsum(pr, axis=0)
    lb_all = cum - cum[0]

    cvec = _pad_rows(jnp.concatenate([c, c_ctx[None, :]], axis=0), 8)
    mods = _ada_mods(cvec, ada_w, ada_b)

    xl = x.reshape(bsz * seq, d)
    xc = ctx.reshape(bsz * ctx_len, d)
    tm_l = PROJ_TM
    tm_c = min(FFN_TM, bsz * ctx_len)
    lat_stream = lambda tm: (lambda i: i // (seq // tm))
    ctx_stream = lambda tm: (lambda i: bsz)
    fw = final_norm_w.reshape(1, d).astype(F32)
    zeros = lambda shape: jnp.zeros(shape, F32)
    wg, wu, wd = (w.astype(BF16) for w in (ffn_w_gate, ffn_w_up, ffn_w_down))
    wp = _permute_w_in(w_in)
    wo = w_out.astype(BF16)

    for l in range(depth):
        need_ctx = l < depth - 1
        mod = mods[l, :n_stream].reshape(n_stream, 3, 3, d)

        def pack(sub):
            nw = jnp.broadcast_to(norm_w[l, sub][None, None, :], (n_stream, 1, d))
            return jnp.concatenate([mod[:, sub], nw, zeros((n_stream, 4, d))], axis=1)

        pk0 = pack(0)
        xl = _ffn(xl, pk0, lat_stream(FFN_TM), wg, wu, wd, (l, 0), fw, tm=FFN_TM, tf=FFN_TF)
        xc = _ffn(xc, pk0, ctx_stream(tm_c), wg, wu, wd, (l, 0), fw, tm=tm_c, tf=FFN_TF)

        pk1 = pack(1)
        s_cw = _pad_rows(ssd_conv_w[l].astype(F32), 8)
        s_cb = ssd_conv_b[l].reshape(1, SSD_XBC).astype(F32)
        m_cw = _pad_rows(ml_conv_w[l].astype(F32), 8)
        m_cb = ml_conv_b[l].reshape(1, 2 * ML_D).astype(F32)
        ucl, ul = _proj(xl, pk1, lat_stream(tm_l), wp, l, m_cw, m_cb, s_cw, s_cb, tm=tm_l, tn=PROJ_TN, seq=seq)
        ucc, uc = _proj(xc, pk1, ctx_stream(ctx_len), wp, l, m_cw, m_cb, s_cw, s_cb, tm=ctx_len, tn=PROJ_TN,
                        seq=ctx_len)
        ucl, ul = (a.reshape(bsz, seq, a.shape[-1]) for a in (ucl, ul))
        ucc, uc = (a.reshape(bsz, ctx_len, a.shape[-1]) for a in (ucc, uc))

        a_neg = -jnp.exp(ssd_a_log[l].astype(F32))
        dv = jnp.repeat(ssd_d[l].astype(F32), SSD_HEAD_DIM).reshape(1, SSD_D)
        hp = jnp.stack([_pad_rows(jnp.stack([_lane_vec([(SM_DT + SSD_HEADS * r, ssd_dt_bias[l, r])]),
                                             _lane_vec([(SM_DT + SSD_HEADS * r, a_neg[r])])]), 8)
                        for r in (0, 1)])
        h0 = zeros((2, bsz, SSD_HEADS // 2, SSD_STATE, 2 * SSD_HEAD_DIM))
        ysc_f, ysc_b, hc = _ssd_scan(ucc, uc, hp, dv, h0)
        ysl_f, ysl_b, _ = _ssd_scan(ucl, ul, hp, dv, hc)

        gb =jnp.stack([_pad_rows(jnp.stack([_lane_vec([(SM_IG + ML_HEADS * r, ml_ig_b[l, r])]),
                                             _lane_vec([(SM_FG + ML_HEADS * r, ml_fg_b[l, r])])]), 8)
                        for r in (0, 1)])
        c0 = zeros((2, bsz, ML_HEADS, ML_HEAD_DIM, 2 * ML_HEAD_DIM))
        m0 = zeros((2, bsz, ML_HEADS, 8, 128))
        ymc_f, ymc_b, cc, mc = _mlstm_scan(ucc, uc, gb, c0, m0)
        yml_f, yml_b, _, _ = _mlstm_scan(ucl, ul, gb, cc, mc)

        lb = lb_all[l].reshape(1, HG_F)
        grid5 = lambda a: a.reshape(bsz, rows, GRID_W // 8, 8, a.shape[-1])
        yh_l, yh_c = [], []
        for rev in (0, 1):
            s0 = zeros((bsz, HG_HEADS, HG_DV, HG_DK))
            yc_, sc = _hgrn_scan(ucc, uc, lb, s0, rev=rev, col_major=False)
            yl_, _ = _hgrn_scan(grid5(ucl), grid5(ul), lb, sc, rev=rev, col_major=True)
            yh_c.append(yc_.reshape(bsz * ctx_len, HG_D))
            yh_l.append(yl_.reshape(bsz * seq, HG_D))

        nw = jnp.concatenate([ssd_norm_w[l], ml_norm_w[l], hg_norm_w[l]]).reshape(1, d).astype(F32)
        flat = lambda a: a.reshape(-1, a.shape[-1])
        xl = _merge(xl, pk1, lat_stream(MERGE_TM), [flat(ysl_f), flat(ysl_b)], [flat(yml_f), flat(yml_b)], yh_l,
                    flat(ul), nw, wo, l, tm=MERGE_TM)

        pk2 = pack(2)
        xl = _ffn(xl, pk2, lat_stream(FFN_TM), wg, wu, wd, (l, 1), fw, tm=FFN_TM, tf=FFN_TF, final=not need_ctx)
        if need_ctx:
            xc = _merge(xc, pk1, ctx_stream(MERGE_TM), [flat(ysc_f), flat(ysc_b)], [flat(ymc_f), flat(ymc_b)], yh_c,
                        flat(uc), nw, wo, l, tm=MERGE_TM)
            xc = _ffn(xc, pk2, ctx_stream(tm_c), wg, wu, wd, (l, 1), fw, tm=tm_c, tf=FFN_TF)

    return xl.reshape(bsz, seq, d)
```
